```python
import jax, jax.numpy as jnp
from jax import lax
import numpy as np

D_MODEL = 1024
BATCH = 2
SEQ = 16384
DEPTH = 1
DEC_BATCH = 32
DEC_SEQ = 32
PAST_LEN = 2048

CHUNK = 64
D_MIX = D_MODEL
W_A = D_MIX // 2
W_B = D_MIX - W_A
N_HEADS_A = 8
CONV_W = 3
HIST_C = CONV_W - 1
POOL_WINDOWS = (2, 4, 8, 16)
N_GROUPS_B = len(POOL_WINDOWS)
GC = W_B // N_GROUPS_B
HIST_P = max(POOL_WINDOWS) - 1
PLE_DIM = 256
PROJ_W = 4 * W_A + 2 * W_B
EPS = 1e-6

kernel_name = "hybrid_shortconv_pool_stream_step"


def rmsnorm(x, g):
    x32 = x.astype(jnp.float32)
    y = x32 * lax.rsqrt(jnp.mean(x32 * x32, axis=-1, keepdims=True) + EPS)
    return (y * g.astype(jnp.float32)).astype(x.dtype)


def short_conv_mixer(hA, bA, cA, zA, hist, conv_w, conv_b):
    T = hA.shape[1]
    u = cA * hA
    uu = jnp.concatenate([hist.astype(u.dtype), u], axis=1)
    c = (uu[:, 0:T] * conv_w[0] + uu[:, 1:1 + T] * conv_w[1]
         + uu[:, 2:2 + T] * conv_w[2] + conv_b)
    y = bA * c * jax.nn.silu(zA)
    return y, uu[:, -HIST_C:]


def pooling_mixer(vB, zB, hist, pos, pool_w, pool_scale):
    Bsz, T, _ = vB.shape
    vv = jnp.concatenate([hist.astype(vB.dtype), vB], axis=1).astype(jnp.float32)
    cs = jnp.concatenate([jnp.zeros((Bsz, 1, W_B), jnp.float32),
                          jnp.cumsum(vv, axis=1)], axis=1)
    v32 = vB.astype(jnp.float32)
    outs = []
    for g, w in enumerate(POOL_WINDOWS):
        sl = slice(g * GC, (g + 1) * GC)
        s = cs[:, HIST_P + 1:HIST_P + 1 + T, sl] - cs[:, HIST_P + 1 - w:HIST_P + 1 - w + T, sl]
        cnt = jnp.minimum(pos + 1, w).astype(jnp.float32)[None, :, None]
        outs.append(s / cnt - v32[:, :, sl])
    pooled = jnp.concatenate(outs, axis=-1).astype(vB.dtype)
    pg = pooled.reshape(Bsz, T, N_GROUPS_B, GC)
    mixed = jnp.einsum('btgc,gcd->btgd', pg, pool_w).reshape(Bsz, T, W_B) * pool_scale
    y = mixed * jax.nn.silu(zB)
    return y, vv[:, -HIST_P:].astype(vB.dtype)


def layer(h, p, conv_hist, pool_hist, pos, g_mix, w_in, conv_w, conv_b, pool_w, pool_scale,
          w_out, g_ple, w_ple_gate, w_ple):
    hn = rmsnorm(h, g_mix)
    proj = hn @ w_in
    hA, bA, cA, zA, vB, zB = jnp.split(
        proj, [W_A, 2 * W_A, 3 * W_A, 4 * W_A, 4 * W_A + W_B], axis=-1)
    yA, new_conv = short_conv_mixer(hA, bA, cA, zA, conv_hist, conv_w, conv_b)
    yB, new_pool = pooling_mixer(vB, zB, pool_hist, pos, pool_w, pool_scale)
    h = h + jnp.concatenate([yA, yB], axis=-1) @ w_out
    gate = jax.nn.sigmoid(rmsnorm(h, g_ple) @ w_ple_gate)
    h = h + (p @ w_ple) * gate
    return h, new_conv, new_pool


def run_trunk(x, p, conv_hist, pool_hist, pos, g_mix, w_in, conv_w, conv_b, pool_w, pool_scale,
              w_out, g_ple, w_ple_gate, w_ple, g_final):
    h = x
    convs, pools = [], []
    for i in range(DEPTH):
        h, nc, npl = layer(h, p[i], conv_hist[i], pool_hist[i], pos, g_mix[i], w_in[i],
                           conv_w[i], conv_b[i], pool_w[i], pool_scale[i], w_out[i],
                           g_ple[i], w_ple_gate[i], w_ple[i])
        convs.append(nc)
        pools.append(npl)
    return rmsnorm(h, g_final), jnp.stack(convs, 0), jnp.stack(pools, 0)


def setup_inputs(seed: int = 0) -> dict:
    key = jax.random.key(seed)
    ks = jax.random.split(key, 20)
    f32 = jnp.float32
    nrm = lambda k, shape, s: jax.random.normal(k, shape, f32) * s
    return {
        "x_prompt": nrm(ks[0], (BATCH, SEQ, D_MODEL), 1.0),
        "x_sample": nrm(ks[1], (DEC_BATCH, DEC_SEQ, D_MODEL), 1.0),
        "cache_conv": nrm(ks[2], (DEPTH, DEC_BATCH, HIST_C, W_A), 1.0),
        "cache_pool": nrm(ks[3], (DEPTH, DEC_BATCH, HIST_P, W_B), 1.0),
        "p_prompt": nrm(ks[4], (DEPTH, BATCH, SEQ, PLE_DIM), 1.0),
        "p_sample": nrm(ks[5], (DEPTH, DEC_BATCH, DEC_SEQ, PLE_DIM), 1.0),
        "g_mix": 1.0 + nrm(ks[6], (DEPTH, D_MODEL), 0.05),
        "w_in": nrm(ks[7], (DEPTH, D_MODEL, PROJ_W), D_MODEL ** -0.5),
        "conv_w": nrm(ks[8], (DEPTH, CONV_W, W_A), CONV_W ** -0.5),
        "conv_b": nrm(ks[9], (DEPTH, W_A), 0.02),
        "pool_w": nrm(ks[10], (DEPTH, N_GROUPS_B, GC, GC), GC ** -0.5),
        "pool_scale": 1.0 + nrm(ks[11], (DEPTH, W_B), 0.1),
        "w_out": nrm(ks[12], (DEPTH, D_MIX, D_MODEL), D_MIX ** -0.5),
        "g_ple": 1.0 + nrm(ks[13], (DEPTH, D_MODEL), 0.05),
        "w_ple_gate": nrm(ks[14], (DEPTH, D_MODEL, D_MODEL), D_MODEL ** -0.5),
        "w_ple": nrm(ks[15], (DEPTH, PLE_DIM, D_MODEL), PLE_DIM ** -0.5),
        "g_final": 1.0 + nrm(ks[16], (D_MODEL,), 0.05),
    }


def reference(x_prompt, x_sample, cache_conv, cache_pool, p_prompt, p_sample, g_mix, w_in,
              conv_w, conv_b, pool_w, pool_scale, w_out, g_ple, w_ple_gate, w_ple, g_final):
    conv0 = jnp.zeros((DEPTH, BATCH, HIST_C, W_A), x_prompt.dtype)
    pool0 = jnp.zeros((DEPTH, BATCH, HIST_P, W_B), x_prompt.dtype)
    pos_prompt = jnp.arange(SEQ, dtype=jnp.int32)
    y_prompt, state_conv_prompt, state_pool_prompt = run_trunk(
        x_prompt, p_prompt, conv0, pool0, pos_prompt, g_mix, w_in, conv_w, conv_b, pool_w,
        pool_scale, w_out, g_ple, w_ple_gate, w_ple, g_final)
    T = x_sample.shape[1]
    pos_sample = PAST_LEN + jnp.arange(T, dtype=jnp.int32)
    y_sample, state_conv_sample, state_pool_sample = run_trunk(
        x_sample, p_sample, cache_conv, cache_pool, pos_sample, g_mix, w_in, conv_w, conv_b,
        pool_w, pool_scale, w_out, g_ple, w_ple_gate, w_ple, g_final)
    return (y_prompt, y_sample, state_conv_prompt, state_pool_prompt, state_conv_sample, state_pool_sample)
```

```python
import functools

import jax
import jax.numpy as jnp
from jax.experimental import pallas as pl
from jax.experimental.pallas import tpu as pltpu

D_MODEL = 1024
W_A = 512
W_B = 512
CONV_W = 3
HIST_C = CONV_W - 1
POOL_WINDOWS = (2, 4, 8, 16)
GC = W_B // len(POOL_WINDOWS)
HIST_P = max(POOL_WINDOWS) - 1
PLE_DIM = 256
PROJ_W = 4 * W_A + 2 * W_B
EPS = 1e-6
PAST_LEN = 2048

SUBLANES = 8
HIST_C_PAD = SUBLANES
HIST_P_PAD = 2 * SUBLANES
PROMPT_TILE = 512
STREAMS_PER_TILE = 8
VMEM_LIMIT_BYTES = 56 * 1024 * 1024


def _rmsnorm(x, g):
    ms = jnp.mean(x * x, axis=-1, keepdims=True)
    return x * jax.lax.rsqrt(ms + EPS) * g


def _silu(z):
    return z / (1.0 + jnp.exp(-z))


def _dot(a, w):
    return jnp.dot(a.astype(jnp.bfloat16), w, preferred_element_type=jnp.float32)


def _shift_rows(ext, k, pad, rows):
    return ext[pad - k:pad - k + rows]


def _conv_mixer(hA, bA, cA, zA, u_hist, conv_w, conv_b):
    rows = hA.shape[0]
    u = cA * hA
    ext = jnp.concatenate([u_hist, u], axis=0)
    c = (_shift_rows(ext, 2, HIST_C_PAD, rows) * conv_w[0:1]
         + _shift_rows(ext, 1, HIST_C_PAD, rows) * conv_w[1:2]
         + u * conv_w[2:3] + conv_b)
    return bA * c * _silu(zA), u


def _pool_group(ext, v, window, pos0):
    rows = v.shape[0]
    total = ext.shape[0]
    s = ext
    span = 1
    while span < window:
        s = s + jnp.concatenate([s[total - span:], s[:total - span]], axis=0)
        span *= 2
    s = s[HIST_P_PAD:]
    head = HIST_P_PAD
    r = jax.lax.broadcasted_iota(jnp.int32, (head, GC), 0)
    cnt = jnp.minimum(pos0 + r + 1, window).astype(jnp.float32)
    mean_head = s[:head] / cnt
    if rows > head:
        mean = jnp.concatenate([mean_head, s[head:] * (1.0 / window)], axis=0)
    else:
        mean = mean_head
    return mean - v


def _pool_mixer(vB, zB, v_hist, pos0, pool_w, pool_scale):
    pooled = []
    for g, window in enumerate(POOL_WINDOWS):
        sl = slice(g * GC, (g + 1) * GC)
        ext = jnp.concatenate([v_hist[:, sl], vB[:, sl]], axis=0)
        pooled.append(_pool_group(ext, vB[:, sl], window, pos0))
    mixed = []
    for pair in range(len(POOL_WINDOWS) // 2):
        pg = jnp.concatenate(pooled[2 * pair:2 * pair + 2], axis=-1)
        mixed.append(_dot(pg, pool_w[pair]))
    mixed = jnp.concatenate(mixed, axis=-1) * pool_scale
    return mixed * _silu(zB)


def _project(x, g_mix, w_in):
    return _dot(_rmsnorm(x, g_mix), w_in)


def _split_proj(proj):
    hA = proj[:, 0 * W_A:1 * W_A]
    bA = proj[:, 1 * W_A:2 * W_A]
    cA = proj[:, 2 * W_A:3 * W_A]
    zA = proj[:, 3 * W_A:4 * W_A]
    vB = proj[:, 4 * W_A:4 * W_A + W_B]
    zB = proj[:, 4 * W_A + W_B:]
    return hA, bA, cA, zA, vB, zB


def _finish(x, p, y_mix, w_out, g_ple, w_gate, w_ple, g_final):
    h = x + _dot(y_mix, w_out)
    z = _dot(_rmsnorm(h, g_ple), w_gate)
    gate = 1.0 / (1.0 + jnp.exp(-z))
    h = h + _dot(p, w_ple) * gate
    return _rmsnorm(h, g_final)


def _prompt_kernel(x_ref, p_ref, g_mix_ref, w_in_ref, conv_w_ref, conv_b_ref, pool_w_ref,
                   pool_scale_ref, w_out_ref, g_ple_ref, w_gate_ref, w_ple_ref, g_final_ref,
                   y_ref, u_tail_ref, v_tail_ref, u_hist_ref, v_hist_ref):
    t = pl.program_id(1)
    rows = x_ref.shape[1]

    @pl.when(t == 0)
    def _():
        u_hist_ref[...] = jnp.zeros_like(u_hist_ref)
        v_hist_ref[...] = jnp.zeros_like(v_hist_ref)

    x = x_ref[0]
    proj = _project(x, g_mix_ref[...], w_in_ref[...])
    hA, bA, cA, zA, vB, zB = _split_proj(proj)
    yA, u = _conv_mixer(hA, bA, cA, zA, u_hist_ref[...], conv_w_ref[...], conv_b_ref[...])
    yB = _pool_mixer(vB, zB, v_hist_ref[...], t * rows, pool_w_ref[...], pool_scale_ref[...])
    u_tail = u[rows - HIST_C_PAD:]
    v_tail = vB[rows - HIST_P_PAD:]
    u_hist_ref[...] = u_tail
    v_hist_ref[...] = v_tail
    u_tail_ref[0] = u_tail
    v_tail_ref[0] = v_tail
    y_mix = jnp.concatenate([yA, yB], axis=-1)
    y_ref[0] = _finish(x, p_ref[0], y_mix, w_out_ref[...], g_ple_ref[...], w_gate_ref[...],
                       w_ple_ref[...], g_final_ref[...])


def _sample_kernel(x_ref, p_ref, u_hist_ref, v_hist_ref, g_mix_ref, w_in_ref, conv_w_ref,
                   conv_b_ref, pool_w_ref, pool_scale_ref, w_out_ref, g_ple_ref, w_gate_ref,
                   w_ple_ref, g_final_ref, y_ref, u_tail_ref, v_tail_ref, *, seq, pos0):
    streams = u_hist_ref.shape[0]
    x = x_ref[...]
    proj = _project(x, g_mix_ref[...], w_in_ref[...])
    hA, bA, cA, zA, vB, zB = _split_proj(proj)
    conv_w = conv_w_ref[...]
    conv_b = conv_b_ref[...]
    pool_w = pool_w_ref[...]
    pool_scale = pool_scale_ref[...]
    y_parts = []
    for s in range(streams):
        sl = slice(s * seq, (s + 1) * seq)
        yA, u = _conv_mixer(hA[sl], bA[sl], cA[sl], zA[sl], u_hist_ref[s], conv_w, conv_b)
        yB = _pool_mixer(vB[sl], zB[sl], v_hist_ref[s], pos0, pool_w, pool_scale)
        u_tail_ref[s] = u[seq - HIST_C_PAD:]
        v_tail_ref[s] = vB[sl][seq - HIST_P_PAD:]
        y_parts.append(jnp.concatenate([yA, yB], axis=-1))
    y_mix = jnp.concatenate(y_parts, axis=0)
    y_ref[...] = _finish(x, p_ref[...], y_mix, w_out_ref[...], g_ple_ref[...], w_gate_ref[...],
                         w_ple_ref[...], g_final_ref[...])


def _resident(shape):
    zeros = (0,) * len(shape)
    return pl.BlockSpec(shape, lambda *_: zeros, pipeline_mode=pl.Buffered(1))


def _weight_specs():
    return [
        _resident((1, D_MODEL)),
        _resident((D_MODEL, PROJ_W)),
        _resident((CONV_W, W_A)),
        _resident((1, W_A)),
        _resident((2, 2 * GC, 2 * GC)),
        _resident((1, W_B)),
        _resident((D_MODEL, D_MODEL)),
        _resident((1, D_MODEL)),
        _resident((D_MODEL, D_MODEL)),
        _resident((PLE_DIM, D_MODEL)),
        _resident((1, D_MODEL)),
    ]


def _run_prompt(x, p, weights):
    batch, seq, _ = x.shape
    tile = PROMPT_TILE
    assert seq % tile == 0 and tile >= HIST_P_PAD
    row_spec = lambda width: pl.BlockSpec((1, tile, width), lambda b, t: (b, t, 0))
    tail_spec = lambda rows, width: pl.BlockSpec((1, rows, width), lambda b, t: (b, 0, 0))
    return pl.pallas_call(
        _prompt_kernel,
        grid=(batch, seq // tile),
        in_specs=[row_spec(D_MODEL), row_spec(PLE_DIM)] + _weight_specs(),
        out_specs=[row_spec(D_MODEL), tail_spec(HIST_C_PAD, W_A), tail_spec(HIST_P_PAD, W_B)],
        out_shape=[
            jax.ShapeDtypeStruct((batch, seq, D_MODEL), jnp.float32),
            jax.ShapeDtypeStruct((batch, HIST_C_PAD, W_A), jnp.float32),
            jax.ShapeDtypeStruct((batch, HIST_P_PAD, W_B), jnp.float32),
        ],
        scratch_shapes=[pltpu.VMEM((HIST_C_PAD, W_A), jnp.float32),
                        pltpu.VMEM((HIST_P_PAD, W_B), jnp.float32)],
        compiler_params=pltpu.CompilerParams(
            dimension_semantics=("arbitrary", "arbitrary"),
            vmem_limit_bytes=VMEM_LIMIT_BYTES),
        name="trunk_prompt",
    )(x, p, *weights)


def _run_sample(x, p, u_hist, v_hist, weights):
    streams, seq, _ = x.shape
    per_tile = STREAMS_PER_TILE
    assert streams % per_tile == 0 and seq >= HIST_P_PAD and seq % SUBLANES == 0
    rows = per_tile * seq
    x2 = x.reshape(streams * seq, D_MODEL)
    p2 = p.reshape(streams * seq, PLE_DIM)
    row_spec = lambda width: pl.BlockSpec((rows, width), lambda i: (i, 0))
    hist_spec = lambda n, width: pl.BlockSpec((per_tile, n, width), lambda i: (i, 0, 0))
    y, u_tail, v_tail = pl.pallas_call(
        functools.partial(_sample_kernel, seq=seq, pos0=PAST_LEN),
        grid=(streams // per_tile,),
        in_specs=[row_spec(D_MODEL), row_spec(PLE_DIM), hist_spec(HIST_C_PAD, W_A),
                  hist_spec(HIST_P_PAD, W_B)] + _weight_specs(),
        out_specs=[row_spec(D_MODEL), hist_spec(HIST_C_PAD, W_A), hist_spec(HIST_P_PAD, W_B)],
        out_shape=[
            jax.ShapeDtypeStruct((streams * seq, D_MODEL), jnp.float32),
            jax.ShapeDtypeStruct((streams, HIST_C_PAD, W_A), jnp.float32),
            jax.ShapeDtypeStruct((streams, HIST_P_PAD, W_B), jnp.float32),
        ],
        compiler_params=pltpu.CompilerParams(
            dimension_semantics=("arbitrary",),
            vmem_limit_bytes=VMEM_LIMIT_BYTES),
        name="trunk_sample",
    )(x2, p2, u_hist, v_hist, *weights)
    return y.reshape(streams, seq, D_MODEL), u_tail, v_tail


def _prepare_weights(g_mix, w_in, conv_w, conv_b, pool_w, pool_scale, w_out, g_ple, w_ple_gate,
                     w_ple, g_final):
    bf16 = jnp.bfloat16
    zero = jnp.zeros((GC, GC), pool_w.dtype)
    pairs = [jnp.block([[pool_w[2 * i], zero], [zero, pool_w[2 * i + 1]]])
             for i in range(len(POOL_WINDOWS) // 2)]
    return (g_mix.reshape(1, D_MODEL), w_in.astype(bf16), conv_w, conv_b.reshape(1, W_A),
            jnp.stack(pairs).astype(bf16), pool_scale.reshape(1, W_B), w_out.astype(bf16),
            g_ple.reshape(1, D_MODEL), w_ple_gate.astype(bf16), w_ple.astype(bf16),
            g_final.reshape(1, D_MODEL))


def _pad_history(cache, rows):
    return jnp.pad(cache, ((0, 0), (rows - cache.shape[1], 0), (0, 0)))


def kernel(x_prompt, x_sample, cache_conv, cache_pool, p_prompt, p_sample, g_mix, w_in, conv_w,
           conv_b, pool_w, pool_scale, w_out, g_ple, w_ple_gate, w_ple, g_final):
    depth = g_mix.shape[0]
    assert depth == 1, "single-layer trunk"
    weights = _prepare_weights(g_mix[0], w_in[0], conv_w[0], conv_b[0], pool_w[0], pool_scale[0],
                               w_out[0], g_ple[0], w_ple_gate[0], w_ple[0], g_final)
    y_prompt, u_tail_p, v_tail_p = _run_prompt(x_prompt, p_prompt[0], weights)
    y_sample, u_tail_s, v_tail_s = _run_sample(
        x_sample, p_sample[0], _pad_history(cache_conv[0], HIST_C_PAD),
        _pad_history(cache_pool[0], HIST_P_PAD), weights)
    return (y_prompt, y_sample,
            u_tail_p[None, :, HIST_C_PAD - HIST_C:], v_tail_p[None, :, HIST_P_PAD - HIST_P:],
            u_tail_s[None, :, HIST_C_PAD - HIST_C:], v_tail_s[None, :, HIST_P_PAD - HIST_P:])
```

```python
import functools

import jax
import jax.numpy as jnp
from jax.experimental import pallas as pl
from jax.experimental.pallas import tpu as pltpu

D_MODEL = 1024
W_A = 512
W_B = 512
CONV_W = 3
HIST_C = CONV_W - 1
POOL_WINDOWS = (2, 4, 8, 16)
GC = W_B // len(POOL_WINDOWS)
HIST_P = max(POOL_WINDOWS) - 1
PLE_DIM = 256
PROJ_W = 4 * W_A + 2 * W_B
EPS = 1e-6
PAST_LEN = 2048

SUBLANES = 8
HIST_C_PAD = SUBLANES
HIST_P_PAD = 2 * SUBLANES
PROMPT_TILE = 1024
PROMPT_SUB_ROWS = 256
STREAMS_PER_TILE = 8
VMEM_LIMIT_BYTES = 56 * 1024 * 1024


def _rmsnorm(x, g):
    ms = jnp.mean(x * x, axis=-1, keepdims=True)
    return x * jax.lax.rsqrt(ms + EPS) * g


def _silu(z):
    return z / (1.0 + jnp.exp(-z))


def _dot(a, w):
    return jnp.dot(a.astype(jnp.bfloat16), w, preferred_element_type=jnp.float32)


def _shift_rows(ext, k, pad, rows):
    return ext[pad - k:pad - k + rows]


def _conv_mixer(hA, bA, cA, zA, u_hist, conv_w, conv_b):
    rows = hA.shape[0]
    u = cA * hA
    ext = jnp.concatenate([u_hist, u], axis=0)
    c = (_shift_rows(ext, 2, HIST_C_PAD, rows) * conv_w[0:1]
         + _shift_rows(ext, 1, HIST_C_PAD, rows) * conv_w[1:2]
         + u * conv_w[2:3] + conv_b)
    return bA * c * _silu(zA), u


def _pool_group(ext, v, window, pos0):
    rows = v.shape[0]
    total = ext.shape[0]
    s = ext
    span = 1
    while span < window:
        s = s + jnp.concatenate([s[total - span:], s[:total - span]], axis=0)
        span *= 2
    s = s[HIST_P_PAD:]
    head = HIST_P_PAD
    r = jax.lax.broadcasted_iota(jnp.int32, (head, GC), 0)
    cnt = jnp.minimum(pos0 + r + 1, window).astype(jnp.float32)
    mean_head = s[:head] / cnt
    if rows > head:
        mean = jnp.concatenate([mean_head, s[head:] * (1.0 / window)], axis=0)
    else:
        mean = mean_head
    return mean - v


def _pooled(vB, v_hist, pos0):
    pooled = []
    for g, window in enumerate(POOL_WINDOWS):
        sl = slice(g * GC, (g + 1) * GC)
        ext = jnp.concatenate([v_hist[:, sl], vB[:, sl]], axis=0)
        pooled.append(_pool_group(ext, vB[:, sl], window, pos0))
    return jnp.concatenate(pooled, axis=-1)


def _split_proj(proj):
    hA = proj[:, 0 * W_A:1 * W_A]
    bA = proj[:, 1 * W_A:2 * W_A]
    cA = proj[:, 2 * W_A:3 * W_A]
    zA = proj[:, 3 * W_A:4 * W_A]
    vB = proj[:, 4 * W_A:4 * W_A + W_B]
    zB = proj[:, 4 * W_A + W_B:]
    return hA, bA, cA, zA, vB, zB


class _Weights:
    def __init__(self, refs):
        (self.g_mix, self.w_in, self.conv_w, self.conv_b, self.pool_w, self.pool_scale,
         self.w_out, self.g_ple, self.w_gate, self.w_ple, self.g_final) = [r[...] for r in refs]


def _segment_stages(w, x, p, hists, pos0, carry, emit):
    s = {}

    def norm_in():
        s["hn"] = _rmsnorm(x, w.g_mix).astype(jnp.bfloat16)

    def project():
        s["proj"] = _dot(s.pop("hn"), w.w_in)

    def mix():
        hA, bA, cA, zA, vB, zB = _split_proj(s.pop("proj"))
        stream_hists = hists()
        seq = x.shape[0] // len(stream_hists)
        yA, pooled, tails = [], [], []
        for i, (u_hist, v_hist) in enumerate(stream_hists):
            sl = slice(i * seq, (i + 1) * seq)
            yA_i, u = _conv_mixer(hA[sl], bA[sl], cA[sl], zA[sl], u_hist, w.conv_w, w.conv_b)
            yA.append(yA_i)
            pooled.append(_pooled(vB[sl], v_hist, pos0))
            tails.append((u[seq - HIST_C_PAD:], vB[sl][seq - HIST_P_PAD:]))
        carry(tails)
        s["yA"] = jnp.concatenate(yA, axis=0).astype(jnp.bfloat16)
        s["pooled"] = jnp.concatenate(pooled, axis=0).astype(jnp.bfloat16)
        s["gB"] = _silu(zB)

    def project_pool():
        pooled = s.pop("pooled")
        s["mixed"] = [_dot(pooled[:, 2 * GC * i:2 * GC * (i + 1)], w.pool_w[i])
                      for i in range(len(POOL_WINDOWS) // 2)]

    def gate_pool():
        yB = jnp.concatenate(s.pop("mixed"), axis=-1) * w.pool_scale * s.pop("gB")
        s["y_mix"] = jnp.concatenate([s.pop("yA"), yB.astype(jnp.bfloat16)], axis=-1)

    def project_out():
        s["o"] = _dot(s.pop("y_mix"), w.w_out)

    def norm_gate():
        s["h"] = x + s.pop("o")
        s["hn2"] = _rmsnorm(s["h"], w.g_ple).astype(jnp.bfloat16)

    def project_gate():
        s["z"] = _dot(s.pop("hn2"), w.w_gate)
        s["e"] = _dot(p, w.w_ple)

    def finish():
        gate = 1.0 / (1.0 + jnp.exp(-s.pop("z")))
        h = s.pop("h") + s.pop("e") * gate
        emit(_rmsnorm(h, w.g_final))

    return [norm_in, project, mix, project_pool, gate_pool, project_out, norm_gate,
            project_gate, finish]


def _run_skewed(stage_lists, skew):
    depth = len(stage_lists[0])
    for slot in range(skew * (len(stage_lists) - 1) + depth):
        for k, stages in enumerate(stage_lists):
            j = slot - skew * k
            if 0 <= j < depth:
                stages[j]()


N_WEIGHTS = 11


def _prompt_kernel(x_ref, p_ref, *refs, sub_rows):
    weight_refs = refs[:N_WEIGHTS]
    y_ref, u_tail_ref, v_tail_ref, u_hist_ref, v_hist_ref = refs[N_WEIGHTS:]
    t = pl.program_id(1)
    rows = x_ref.shape[1]

    @pl.when(t == 0)
    def _():
        u_hist_ref[...] = jnp.zeros_like(u_hist_ref)
        v_hist_ref[...] = jnp.zeros_like(v_hist_ref)

    w = _Weights(weight_refs)
    tails = [(u_hist_ref[...], v_hist_ref[...])]
    stage_lists = []
    for k in range(rows // sub_rows):
        sl = pl.ds(k * sub_rows, sub_rows)

        def emit(out, sl=sl):
            y_ref[0, sl, :] = out

        stage_lists.append(_segment_stages(
            w, x_ref[0, sl, :], p_ref[0, sl, :], lambda k=k: [tails[k]],
            t * rows + k * sub_rows, lambda new: tails.append(new[0]), emit))
    _run_skewed(stage_lists, skew=2)
    u_tail, v_tail = tails[-1]
    u_hist_ref[...] = u_tail
    v_hist_ref[...] = v_tail
    u_tail_ref[0] = u_tail
    v_tail_ref[0] = v_tail


def _sample_kernel(x_ref, p_ref, u_hist_ref, v_hist_ref, *refs, pos0):
    weight_refs = refs[:N_WEIGHTS]
    y_ref, u_tail_ref, v_tail_ref = refs[N_WEIGHTS:]
    streams = u_hist_ref.shape[0]

    def carry(tails):
        for i, (u_tail, v_tail) in enumerate(tails):
            u_tail_ref[i] = u_tail
            v_tail_ref[i] = v_tail

    def emit(out):
        y_ref[...] = out

    stages = _segment_stages(
        _Weights(weight_refs), x_ref[...], p_ref[...],
        lambda: [(u_hist_ref[i], v_hist_ref[i]) for i in range(streams)], pos0, carry, emit)
    _run_skewed([stages], skew=1)


def _resident(shape):
    zeros = (0,) * len(shape)
    return pl.BlockSpec(shape, lambda *_: zeros, pipeline_mode=pl.Buffered(1))


def _weight_specs():
    specs = [
        _resident((1, D_MODEL)),
        _resident((D_MODEL, PROJ_W)),
        _resident((CONV_W, W_A)),
        _resident((1, W_A)),
        _resident((2, 2 * GC, 2 * GC)),
        _resident((1, W_B)),
        _resident((D_MODEL, D_MODEL)),
        _resident((1, D_MODEL)),
        _resident((D_MODEL, D_MODEL)),
        _resident((PLE_DIM, D_MODEL)),
        _resident((1, D_MODEL)),
    ]
    assert len(specs) == N_WEIGHTS
    return specs


def _run_prompt(x, p, weights):
    batch, seq, _ = x.shape
    tile, sub_rows = PROMPT_TILE, PROMPT_SUB_ROWS
    assert seq % tile == 0 and tile % sub_rows == 0 and sub_rows >= HIST_P_PAD
    row_spec = lambda width: pl.BlockSpec((1, tile, width), lambda b, t: (b, t, 0))
    tail_spec = lambda rows, width: pl.BlockSpec((1, rows, width), lambda b, t: (b, 0, 0))
    return pl.pallas_call(
        functools.partial(_prompt_kernel, sub_rows=sub_rows),
        grid=(batch, seq // tile),
        in_specs=[row_spec(D_MODEL), row_spec(PLE_DIM)] + _weight_specs(),
        out_specs=[row_spec(D_MODEL), tail_spec(HIST_C_PAD, W_A), tail_spec(HIST_P_PAD, W_B)],
        out_shape=[
            jax.ShapeDtypeStruct((batch, seq, D_MODEL), jnp.float32),
            jax.ShapeDtypeStruct((batch, HIST_C_PAD, W_A), jnp.float32),
            jax.ShapeDtypeStruct((batch, HIST_P_PAD, W_B), jnp.float32),
        ],
        scratch_shapes=[pltpu.VMEM((HIST_C_PAD, W_A), jnp.float32),
                        pltpu.VMEM((HIST_P_PAD, W_B), jnp.float32)],
        compiler_params=pltpu.CompilerParams(
            dimension_semantics=("arbitrary", "arbitrary"),
            vmem_limit_bytes=VMEM_LIMIT_BYTES),
        name="trunk_prompt",
    )(x, p, *weights)


def _run_sample(x, p, u_hist, v_hist, weights):
    streams, seq, _ = x.shape
    per_tile = STREAMS_PER_TILE
    assert streams % per_tile == 0 and seq >= HIST_P_PAD and seq % SUBLANES == 0
    rows = per_tile * seq
    x2 = x.reshape(streams * seq, D_MODEL)
    p2 = p.reshape(streams * seq, PLE_DIM)
    row_spec = lambda width: pl.BlockSpec((rows, width), lambda i: (i, 0))
    hist_spec = lambda n, width: pl.BlockSpec((per_tile, n, width), lambda i: (i, 0, 0))
    y, u_tail, v_tail = pl.pallas_call(
        functools.partial(_sample_kernel, pos0=PAST_LEN),
        grid=(streams // per_tile,),
        in_specs=[row_spec(D_MODEL), row_spec(PLE_DIM), hist_spec(HIST_C_PAD, W_A),
                  hist_spec(HIST_P_PAD, W_B)] + _weight_specs(),
        out_specs=[row_spec(D_MODEL), hist_spec(HIST_C_PAD, W_A), hist_spec(HIST_P_PAD, W_B)],
        out_shape=[
            jax.ShapeDtypeStruct((streams * seq, D_MODEL), jnp.float32),
            jax.ShapeDtypeStruct((streams, HIST_C_PAD, W_A), jnp.float32),
            jax.ShapeDtypeStruct((streams, HIST_P_PAD, W_B), jnp.float32),
        ],
        compiler_params=pltpu.CompilerParams(
            dimension_semantics=("arbitrary",),
            vmem_limit_bytes=VMEM_LIMIT_BYTES),
        name="trunk_sample",
    )(x2, p2, u_hist, v_hist, *weights)
    return y.reshape(streams, seq, D_MODEL), u_tail, v_tail


def _prepare_weights(g_mix, w_in, conv_w, conv_b, pool_w, pool_scale, w_out, g_ple, w_ple_gate,
                     w_ple, g_final):
    bf16 = jnp.bfloat16
    zero = jnp.zeros((GC, GC), pool_w.dtype)
    pairs = [jnp.block([[pool_w[2 * i], zero], [zero, pool_w[2 * i + 1]]])
             for i in range(len(POOL_WINDOWS) // 2)]
    return (g_mix.reshape(1, D_MODEL), w_in.astype(bf16), conv_w, conv_b.reshape(1, W_A),
            jnp.stack(pairs).astype(bf16), pool_scale.reshape(1, W_B), w_out.astype(bf16),
            g_ple.reshape(1, D_MODEL), w_ple_gate.astype(bf16), w_ple.astype(bf16),
            g_final.reshape(1, D_MODEL))


def _pad_history(cache, rows):
    return jnp.pad(cache, ((0, 0), (rows - cache.shape[1], 0), (0, 0)))


def kernel(x_prompt, x_sample, cache_conv, cache_pool, p_prompt, p_sample, g_mix, w_in, conv_w,
           conv_b, pool_w, pool_scale, w_out, g_ple, w_ple_gate, w_ple, g_final):
    depth = g_mix.shape[0]
    assert depth == 1, "single-layer trunk"
    weights = _prepare_weights(g_mix[0], w_in[0], conv_w[0], conv_b[0], pool_w[0], pool_scale[0],
                               w_out[0], g_ple[0], w_ple_gate[0], w_ple[0], g_final)
    y_prompt, u_tail_p, v_tail_p = _run_prompt(x_prompt, p_prompt[0], weights)
    y_sample, u_tail_s, v_tail_s = _run_sample(
        x_sample, p_sample[0], _pad_history(cache_conv[0], HIST_C_PAD),
        _pad_history(cache_pool[0], HIST_P_PAD), weights)
    return (y_prompt, y_sample,
            u_tail_p[None, :, HIST_C_PAD - HIST_C:], v_tail_p[None, :, HIST_P_PAD - HIST_P:],
            u_tail_s[None, :, HIST_C_PAD - HIST_C:], v_tail_s[None, :, HIST_P_PAD - HIST_P:])
```

```python
import functools

import jax
import jax.numpy as jnp
from jax.experimental import pallas as pl
from jax.experimental.pallas import tpu as pltpu

D_MODEL = 1024
W_A = 512
W_B = 512
CONV_W = 3
HIST_C = CONV_W - 1
POOL_WINDOWS = (2, 4, 8, 16)
GC = W_B // len(POOL_WINDOWS)
HIST_P = max(POOL_WINDOWS) - 1
PLE_DIM = 256
PROJ_W = 4 * W_A + 2 * W_B
EPS = 1e-6
PAST_LEN = 2048
LOG2_E = 1.4426950408889634

SUBLANES = 8
HIST_C_PAD = SUBLANES
HIST_P_PAD = 2 * SUBLANES
PROMPT_TILE = 1024
PROMPT_SUB_ROWS = 256
STREAMS_PER_TILE = 8
VMEM_LIMIT_BYTES = 56 * 1024 * 1024


def _rmsnorm(x, g):
    ms = jnp.mean(x * x, axis=-1, keepdims=True)
    return x * jax.lax.rsqrt(ms + EPS) * g


def _sigmoid(z):
    return 1.0 / (1.0 + jnp.exp2(z * -LOG2_E))


def _silu(z):
    return z / (1.0 + jnp.exp2(z * -LOG2_E))


def _dot(a, w):
    return jnp.dot(a.astype(jnp.bfloat16), w, preferred_element_type=jnp.float32)


def _shift_rows(ext, k, pad):
    return pltpu.roll(ext, k, axis=0)[pad:]


def _conv_mixer(hA, bA, cA, zA, u_hist, conv_w, conv_b):
    u = cA * hA
    ext = jnp.concatenate([u_hist, u], axis=0)
    c = (_shift_rows(ext, 2, HIST_C_PAD) * conv_w[0:1]
         + _shift_rows(ext, 1, HIST_C_PAD) * conv_w[1:2]
         + u * conv_w[2:3] + conv_b)
    return bA * c * _silu(zA), u


def _pool_group(ext, v, window, pos0):
    rows = v.shape[0]
    total = ext.shape[0]
    s = ext
    span = 1
    while span < window:
        s = s + jnp.concatenate([s[total - span:], s[:total - span]], axis=0)
        span *= 2
    s = s[HIST_P_PAD:]
    head = HIST_P_PAD
    r = jax.lax.broadcasted_iota(jnp.int32, (head, GC), 0)
    cnt = jnp.minimum(pos0 + r + 1, window).astype(jnp.float32)
    mean_head = s[:head] / cnt
    if rows > head:
        mean = jnp.concatenate([mean_head, s[head:] * (1.0 / window)], axis=0)
    else:
        mean = mean_head
    return mean - v


def _pooled(vB, v_hist, pos0):
    pooled = []
    for g, window in enumerate(POOL_WINDOWS):
        sl = slice(g * GC, (g + 1) * GC)
        ext = jnp.concatenate([v_hist[:, sl], vB[:, sl]], axis=0)
        pooled.append(_pool_group(ext, vB[:, sl], window, pos0))
    return jnp.concatenate(pooled, axis=-1)


def _split_proj(proj):
    hA = proj[:, 0 * W_A:1 * W_A]
    bA = proj[:, 1 * W_A:2 * W_A]
    cA = proj[:, 2 * W_A:3 * W_A]
    zA = proj[:, 3 * W_A:4 * W_A]
    vB = proj[:, 4 * W_A:4 * W_A + W_B]
    zB = proj[:, 4 * W_A + W_B:]
    return hA, bA, cA, zA, vB, zB


class _Weights:
    def __init__(self, refs):
        (self.g_mix, self.w_in, self.conv_w, self.conv_b, self.pool_w, self.pool_scale,
         self.w_out, self.g_ple, self.w_gate, self.w_ple, self.g_final) = [r[...] for r in refs]


def _segment_stages(w, x, p, hists, pos0, carry, emit):
    s = {}

    def norm_in():
        s["hn"] = _rmsnorm(x, w.g_mix).astype(jnp.bfloat16)

    def project():
        s["proj"] = _dot(s.pop("hn"), w.w_in)

    def mix():
        hA, bA, cA, zA, vB, zB = _split_proj(s.pop("proj"))
        stream_hists = hists()
        seq = x.shape[0] // len(stream_hists)
        yA, pooled, tails = [], [], []
        for i, (u_hist, v_hist) in enumerate(stream_hists):
            sl = slice(i * seq, (i + 1) * seq)
            yA_i, u = _conv_mixer(hA[sl], bA[sl], cA[sl], zA[sl], u_hist, w.conv_w, w.conv_b)
            yA.append(yA_i)
            pooled.append(_pooled(vB[sl], v_hist, pos0))
            tails.append((u[seq - HIST_C_PAD:], vB[sl][seq - HIST_P_PAD:]))
        carry(tails)
        s["yA"] = jnp.concatenate(yA, axis=0).astype(jnp.bfloat16)
        s["pooled"] = jnp.concatenate(pooled, axis=0).astype(jnp.bfloat16)
        s["gB"] = _silu(zB)

    def project_pool():
        pooled = s.pop("pooled")
        s["mixed"] = [_dot(pooled[:, 2 * GC * i:2 * GC * (i + 1)], w.pool_w[i])
                      for i in range(len(POOL_WINDOWS) // 2)]

    def gate_pool():
        yB = jnp.concatenate(s.pop("mixed"), axis=-1) * w.pool_scale * s.pop("gB")
        s["y_mix"] = jnp.concatenate([s.pop("yA"), yB.astype(jnp.bfloat16)], axis=-1)

    def project_out():
        s["o"] = _dot(s.pop("y_mix"), w.w_out)

    def norm_gate():
        s["h"] = x + s.pop("o")
        s["hn2"] = _rmsnorm(s["h"], w.g_ple).astype(jnp.bfloat16)

    def project_gate():
        s["z"] = _dot(s.pop("hn2"), w.w_gate)
        s["e"] = _dot(p, w.w_ple)

    def finish():
        h = s.pop("h") + s.pop("e") * _sigmoid(s.pop("z"))
        emit(_rmsnorm(h, w.g_final))

    return [norm_in, project, mix, project_pool, gate_pool, project_out, norm_gate,
            project_gate, finish]


def _run_skewed(stage_lists, skew):
    depth = len(stage_lists[0])
    for slot in range(skew * (len(stage_lists) - 1) + depth):
        for k, stages in enumerate(stage_lists):
            j = slot - skew * k
            if 0 <= j < depth:
                stages[j]()


N_WEIGHTS = 11


def _prompt_kernel(x_ref, p_ref, *refs, sub_rows):
    weight_refs = refs[:N_WEIGHTS]
    y_ref, u_tail_ref, v_tail_ref, u_hist_ref, v_hist_ref = refs[N_WEIGHTS:]
    t = pl.program_id(1)
    rows = x_ref.shape[1]

    @pl.when(t == 0)
    def _():
        u_hist_ref[...] = jnp.zeros_like(u_hist_ref)
        v_hist_ref[...] = jnp.zeros_like(v_hist_ref)

    w = _Weights(weight_refs)
    tails = [(u_hist_ref[...], v_hist_ref[...])]
    stage_lists = []
    for k in range(rows // sub_rows):
        sl = pl.ds(k * sub_rows, sub_rows)

        def emit(out, sl=sl):
            y_ref[0, sl, :] = out

        stage_lists.append(_segment_stages(
            w, x_ref[0, sl, :], p_ref[0, sl, :], lambda k=k: [tails[k]],
            t * rows + k * sub_rows, lambda new: tails.append(new[0]), emit))
    _run_skewed(stage_lists, skew=2)
    u_tail, v_tail = tails[-1]
    u_hist_ref[...] = u_tail
    v_hist_ref[...] = v_tail
    u_tail_ref[0] = u_tail
    v_tail_ref[0] = v_tail


def _sample_kernel(x_ref, p_ref, u_hist_ref, v_hist_ref, *refs, pos0):
    weight_refs = refs[:N_WEIGHTS]
    y_ref, u_tail_ref, v_tail_ref = refs[N_WEIGHTS:]
    streams = u_hist_ref.shape[0]

    def carry(tails):
        for i, (u_tail, v_tail) in enumerate(tails):
            u_tail_ref[i] = u_tail
            v_tail_ref[i] = v_tail

    def emit(out):
        y_ref[...] = out

    stages = _segment_stages(
        _Weights(weight_refs), x_ref[...], p_ref[...],
        lambda: [(u_hist_ref[i], v_hist_ref[i]) for i in range(streams)], pos0, carry, emit)
    _run_skewed([stages], skew=1)


def _resident(shape):
    zeros = (0,) * len(shape)
    return pl.BlockSpec(shape, lambda *_: zeros, pipeline_mode=pl.Buffered(1))


def _weight_specs():
    specs = [
        _resident((1, D_MODEL)),
        _resident((D_MODEL, PROJ_W)),
        _resident((CONV_W, W_A)),
        _resident((1, W_A)),
        _resident((2, 2 * GC, 2 * GC)),
        _resident((1, W_B)),
        _resident((D_MODEL, D_MODEL)),
        _resident((1, D_MODEL)),
        _resident((D_MODEL, D_MODEL)),
        _resident((PLE_DIM, D_MODEL)),
        _resident((1, D_MODEL)),
    ]
    assert len(specs) == N_WEIGHTS
    return specs


def _run_prompt(x, p, weights):
    batch, seq, _ = x.shape
    tile, sub_rows = PROMPT_TILE, PROMPT_SUB_ROWS
    assert seq % tile == 0 and tile % sub_rows == 0 and sub_rows >= HIST_P_PAD
    row_spec = lambda width: pl.BlockSpec((1, tile, width), lambda b, t: (b, t, 0))
    tail_spec = lambda rows, width: pl.BlockSpec((1, rows, width), lambda b, t: (b, 0, 0))
    return pl.pallas_call(
        functools.partial(_prompt_kernel, sub_rows=sub_rows),
        grid=(batch, seq // tile),
        in_specs=[row_spec(D_MODEL), row_spec(PLE_DIM)] + _weight_specs(),
        out_specs=[row_spec(D_MODEL), tail_spec(HIST_C_PAD, W_A), tail_spec(HIST_P_PAD, W_B)],
        out_shape=[
            jax.ShapeDtypeStruct((batch, seq, D_MODEL), jnp.float32),
            jax.ShapeDtypeStruct((batch, HIST_C_PAD, W_A), jnp.float32),
            jax.ShapeDtypeStruct((batch, HIST_P_PAD, W_B), jnp.float32),
        ],
        scratch_shapes=[pltpu.VMEM((HIST_C_PAD, W_A), jnp.float32),
                        pltpu.VMEM((HIST_P_PAD, W_B), jnp.float32)],
        compiler_params=pltpu.CompilerParams(
            dimension_semantics=("arbitrary", "arbitrary"),
            vmem_limit_bytes=VMEM_LIMIT_BYTES),
        name="trunk_prompt",
    )(x, p, *weights)


def _run_sample(x, p, u_hist, v_hist, weights):
    streams, seq, _ = x.shape
    per_tile = STREAMS_PER_TILE
    assert streams % per_tile == 0 and seq >= HIST_P_PAD and seq % SUBLANES == 0
    rows = per_tile * seq
    x2 = x.reshape(streams * seq, D_MODEL)
    p2 = p.reshape(streams * seq, PLE_DIM)
    row_spec = lambda width: pl.BlockSpec((rows, width), lambda i: (i, 0))
    hist_spec = lambda n, width: pl.BlockSpec((per_tile, n, width), lambda i: (i, 0, 0))
    y, u_tail, v_tail = pl.pallas_call(
        functools.partial(_sample_kernel, pos0=PAST_LEN),
        grid=(streams // per_tile,),
        in_specs=[row_spec(D_MODEL), row_spec(PLE_DIM), hist_spec(HIST_C_PAD, W_A),
                  hist_spec(HIST_P_PAD, W_B)] + _weight_specs(),
        out_specs=[row_spec(D_MODEL), hist_spec(HIST_C_PAD, W_A), hist_spec(HIST_P_PAD, W_B)],
        out_shape=[
            jax.ShapeDtypeStruct((streams * seq, D_MODEL), jnp.float32),
            jax.ShapeDtypeStruct((streams, HIST_C_PAD, W_A), jnp.float32),
            jax.ShapeDtypeStruct((streams, HIST_P_PAD, W_B), jnp.float32),
        ],
        compiler_params=pltpu.CompilerParams(
            dimension_semantics=("arbitrary",),
            vmem_limit_bytes=VMEM_LIMIT_BYTES),
        name="trunk_sample",
    )(x2, p2, u_hist, v_hist, *weights)
    return y.reshape(streams, seq, D_MODEL), u_tail, v_tail


def _prepare_weights(g_mix, w_in, conv_w, conv_b, pool_w, pool_scale, w_out, g_ple, w_ple_gate,
                     w_ple, g_final):
    bf16 = jnp.bfloat16
    zero = jnp.zeros((GC, GC), pool_w.dtype)
    pairs = [jnp.block([[pool_w[2 * i], zero], [zero, pool_w[2 * i + 1]]])
             for i in range(len(POOL_WINDOWS) // 2)]
    return (g_mix.reshape(1, D_MODEL), w_in.astype(bf16), conv_w, conv_b.reshape(1, W_A),
            jnp.stack(pairs).astype(bf16), pool_scale.reshape(1, W_B), w_out.astype(bf16),
            g_ple.reshape(1, D_MODEL), w_ple_gate.astype(bf16), w_ple.astype(bf16),
            g_final.reshape(1, D_MODEL))


def _pad_history(cache, rows):
    return jnp.pad(cache, ((0, 0), (rows - cache.shape[1], 0), (0, 0)))


def kernel(x_prompt, x_sample, cache_conv, cache_pool, p_prompt, p_sample, g_mix, w_in, conv_w,
           conv_b, pool_w, pool_scale, w_out, g_ple, w_ple_gate, w_ple, g_final):
    depth = g_mix.shape[0]
    assert depth == 1, "single-layer trunk"
    weights = _prepare_weights(g_mix[0], w_in[0], conv_w[0], conv_b[0], pool_w[0], pool_scale[0],
                               w_out[0], g_ple[0], w_ple_gate[0], w_ple[0], g_final)
    y_prompt, u_tail_p, v_tail_p = _run_prompt(x_prompt, p_prompt[0], weights)
    y_sample, u_tail_s, v_tail_s = _run_sample(
        x_sample, p_sample[0], _pad_history(cache_conv[0], HIST_C_PAD),
        _pad_history(cache_pool[0], HIST_P_PAD), weights)
    return (y_prompt, y_sample,
            u_tail_p[None, :, HIST_C_PAD - HIST_C:], v_tail_p[None, :, HIST_P_PAD - HIST_P:],
            u_tail_s[None, :, HIST_C_PAD - HIST_C:], v_tail_s[None, :, HIST_P_PAD - HIST_P:])
```

```python
import functools

import jax
import jax.numpy as jnp
from jax.experimental import pallas as pl
from jax.experimental.pallas import tpu as pltpu

D_MODEL = 1024
W_A = 512
W_B = 512
CONV_W = 3
HIST_C = CONV_W - 1
POOL_WINDOWS = (2, 4, 8, 16)
GC = W_B // len(POOL_WINDOWS)
HIST_P = max(POOL_WINDOWS) - 1
PLE_DIM = 256
PROJ_W = 4 * W_A + 2 * W_B
EPS = 1e-6
PAST_LEN = 2048
LOG2_E = 1.4426950408889634

SUBLANES = 8
HIST_C_PAD = SUBLANES
HIST_P_PAD = 2 * SUBLANES
PROMPT_TILE = 1024
PROMPT_SUB_ROWS = 256
VMEM_LIMIT_BYTES = 56 * 1024 * 1024


def _rmsnorm(x, g):
    ms = jnp.mean(x * x, axis=-1, keepdims=True)
    return x * jax.lax.rsqrt(ms + EPS) * g


def _sigmoid(z):
    return 1.0 / (1.0 + jnp.exp2(z * -LOG2_E))


def _silu(z):
    return z / (1.0 + jnp.exp2(z * -LOG2_E))


def _dot(a, w):
    return jnp.dot(a.astype(jnp.bfloat16), w, preferred_element_type=jnp.float32)


def _shift_rows(ext, k, pad):
    return pltpu.roll(ext, k, axis=0)[pad:]


def _conv_mixer(hA, bA, cA, zA, u_hist, conv_w, conv_b):
    u = cA * hA
    ext = jnp.concatenate([u_hist, u], axis=0)
    c = (_shift_rows(ext, 2, HIST_C_PAD) * conv_w[0:1]
         + _shift_rows(ext, 1, HIST_C_PAD) * conv_w[1:2]
         + u * conv_w[2:3] + conv_b)
    return bA * c * _silu(zA), u


def _pool_group(ext, v, window, pos0):
    rows = v.shape[0]
    total = ext.shape[0]
    s = ext
    span = 1
    while span < window:
        s = s + jnp.concatenate([s[total - span:], s[:total - span]], axis=0)
        span *= 2
    s = s[HIST_P_PAD:]
    head = HIST_P_PAD
    r = jax.lax.broadcasted_iota(jnp.int32, (head, GC), 0)
    cnt = jnp.minimum(pos0 + r + 1, window).astype(jnp.float32)
    mean_head = s[:head] / cnt
    if rows > head:
        mean = jnp.concatenate([mean_head, s[head:] * (1.0 / window)], axis=0)
    else:
        mean = mean_head
    return mean - v


def _pooled(vB, v_hist, pos0):
    pooled = []
    for g, window in enumerate(POOL_WINDOWS):
        sl = slice(g * GC, (g + 1) * GC)
        ext = jnp.concatenate([v_hist[:, sl], vB[:, sl]], axis=0)
        pooled.append(_pool_group(ext, vB[:, sl], window, pos0))
    return jnp.concatenate(pooled, axis=-1)


def _split_proj(proj):
    hA = proj[:, 0 * W_A:1 * W_A]
    bA = proj[:, 1 * W_A:2 * W_A]
    cA = proj[:, 2 * W_A:3 * W_A]
    zA = proj[:, 3 * W_A:4 * W_A]
    vB = proj[:, 4 * W_A:4 * W_A + W_B]
    zB = proj[:, 4 * W_A + W_B:]
    return hA, bA, cA, zA, vB, zB


class _Weights:
    def __init__(self, refs):
        (self.g_mix, self.w_in, self.conv_w, self.conv_b, self.pool_w, self.pool_scale,
         self.w_out, self.g_ple, self.w_gate, self.w_ple, self.g_final) = [r[...] for r in refs]


def _segment_stages(w, x, p, hists, pos0, carry, emit):
    s = {}

    def norm_in():
        s["hn"] = _rmsnorm(x, w.g_mix).astype(jnp.bfloat16)

    def project():
        s["proj"] = _dot(s.pop("hn"), w.w_in)

    def mix():
        hA, bA, cA, zA, vB, zB = _split_proj(s.pop("proj"))
        stream_hists = hists()
        seq = x.shape[0] // len(stream_hists)
        yA, pooled, tails = [], [], []
        for i, (u_hist, v_hist) in enumerate(stream_hists):
            sl = slice(i * seq, (i + 1) * seq)
            yA_i, u = _conv_mixer(hA[sl], bA[sl], cA[sl], zA[sl], u_hist, w.conv_w, w.conv_b)
            yA.append(yA_i)
            pooled.append(_pooled(vB[sl], v_hist, pos0))
            tails.append((u[seq - HIST_C_PAD:], vB[sl][seq - HIST_P_PAD:]))
        carry(tails)
        s["yA"] = jnp.concatenate(yA, axis=0).astype(jnp.bfloat16)
        s["pooled"] = jnp.concatenate(pooled, axis=0).astype(jnp.bfloat16)
        s["gB"] = _silu(zB)

    def project_pool():
        pooled = s.pop("pooled")
        s["mixed"] = [_dot(pooled[:, 2 * GC * i:2 * GC * (i + 1)], w.pool_w[i])
                      for i in range(len(POOL_WINDOWS) // 2)]

    def gate_pool():
        yB = jnp.concatenate(s.pop("mixed"), axis=-1) * w.pool_scale * s.pop("gB")
        s["y_mix"] = jnp.concatenate([s.pop("yA"), yB.astype(jnp.bfloat16)], axis=-1)

    def project_out():
        s["o"] = _dot(s.pop("y_mix"), w.w_out)

    def norm_gate():
        s["h"] = x + s.pop("o")
        s["hn2"] = _rmsnorm(s["h"], w.g_ple).astype(jnp.bfloat16)

    def project_gate():
        s["z"] = _dot(s.pop("hn2"), w.w_gate)
        s["e"] = _dot(p, w.w_ple)

    def finish():
        h = s.pop("h") + s.pop("e") * _sigmoid(s.pop("z"))
        emit(_rmsnorm(h, w.g_final))

    return [norm_in, project, mix, project_pool, gate_pool, project_out, norm_gate,
            project_gate, finish]


def _run_skewed(stage_lists, skew):
    depth = len(stage_lists[0])
    for slot in range(skew * (len(stage_lists) - 1) + depth):
        for k, stages in enumerate(stage_lists):
            j = slot - skew * k
            if 0 <= j < depth:
                stages[j]()


N_WEIGHTS = 11


def _trunk_kernel(x_ref, p_ref, xs_ref, ps_ref, cache_conv_ref, cache_pool_ref, *refs,
                  sub_rows, tiles_per_seq, n_prompt_tiles, sample_seq, sample_pos0):
    weight_refs = refs[:N_WEIGHTS]
    (y_ref, ys_ref, conv_p_ref, pool_p_ref, conv_s_ref, pool_s_ref,
     u_hist_ref, v_hist_ref) = refs[N_WEIGHTS:]
    t = pl.program_id(0)
    tile_in_seq = jax.lax.rem(t, tiles_per_seq)

    @pl.when(jnp.logical_and(t < n_prompt_tiles, tile_in_seq == 0))
    def _():
        u_hist_ref[...] = jnp.zeros_like(u_hist_ref)
        v_hist_ref[...] = jnp.zeros_like(v_hist_ref)

    @pl.when(t < n_prompt_tiles)
    def _():
        rows = x_ref.shape[0]
        w = _Weights(weight_refs)
        tails = [(u_hist_ref[...], v_hist_ref[...])]
        stage_lists = []
        for k in range(rows // sub_rows):
            sl = pl.ds(k * sub_rows, sub_rows)

            def emit(out, sl=sl):
                y_ref[sl, :] = out

            stage_lists.append(_segment_stages(
                w, x_ref[sl, :], p_ref[sl, :], lambda k=k: [tails[k]],
                tile_in_seq * rows + k * sub_rows, lambda new: tails.append(new[0]), emit))
        _run_skewed(stage_lists, skew=2)
        u_tail, v_tail = tails[-1]
        u_hist_ref[...] = u_tail
        v_hist_ref[...] = v_tail
        seq = jax.lax.div(t, tiles_per_seq)
        conv_p_ref[seq] = u_tail[HIST_C_PAD - HIST_C:]
        pool_p_ref[seq] = v_tail[HIST_P_PAD - HIST_P:]

    @pl.when(t >= n_prompt_tiles)
    def _():
        streams = xs_ref.shape[0] // sample_seq

        def hists():
            zeros_c = jnp.zeros((HIST_C_PAD - HIST_C, W_A), jnp.float32)
            zeros_p = jnp.zeros((HIST_P_PAD - HIST_P, W_B), jnp.float32)
            return [(jnp.concatenate([zeros_c, cache_conv_ref[i]], axis=0),
                     jnp.concatenate([zeros_p, cache_pool_ref[i]], axis=0))
                    for i in range(streams)]

        def carry(tails):
            for i, (u_tail, v_tail) in enumerate(tails):
                conv_s_ref[i] = u_tail[HIST_C_PAD - HIST_C:]
                pool_s_ref[i] = v_tail[HIST_P_PAD - HIST_P:]

        def emit(out):
            ys_ref[...] = out

        stages = _segment_stages(_Weights(weight_refs), xs_ref[...], ps_ref[...], hists,
                                 sample_pos0, carry, emit)
        _run_skewed([stages], skew=1)


def _resident(shape):
    zeros = (0,) * len(shape)
    return pl.BlockSpec(shape, lambda *_: zeros, pipeline_mode=pl.Buffered(1))


def _weight_specs():
    specs = [
        _resident((1, D_MODEL)),
        _resident((D_MODEL, PROJ_W)),
        _resident((CONV_W, W_A)),
        _resident((1, W_A)),
        _resident((2, 2 * GC, 2 * GC)),
        _resident((1, W_B)),
        _resident((D_MODEL, D_MODEL)),
        _resident((1, D_MODEL)),
        _resident((D_MODEL, D_MODEL)),
        _resident((PLE_DIM, D_MODEL)),
        _resident((1, D_MODEL)),
    ]
    assert len(specs) == N_WEIGHTS
    return specs


def _run_trunk(x, p, xs, ps, cache_conv, cache_pool, weights):
    batch, seq, _ = x.shape
    streams, sample_seq, _ = xs.shape
    tile, sub_rows = PROMPT_TILE, PROMPT_SUB_ROWS
    assert seq % tile == 0 and tile % sub_rows == 0 and sub_rows >= HIST_P_PAD
    assert sub_rows % sample_seq == 0 and sample_seq >= HIST_P_PAD and sample_seq % SUBLANES == 0
    per_tile = sub_rows // sample_seq
    assert streams % per_tile == 0
    n_prompt_tiles = batch * seq // tile
    n_sample_tiles = streams // per_tile
    prompt_tile = lambda t: (jnp.minimum(t, n_prompt_tiles - 1), 0)
    sample_tile = lambda t: (jnp.clip(t - n_prompt_tiles, 0, n_sample_tiles - 1), 0)
    sample_streams = lambda t: sample_tile(t) + (0,)
    whole = lambda t: (0, 0, 0)
    f32 = jnp.float32
    y, ys, conv_p, pool_p, conv_s, pool_s = pl.pallas_call(
        functools.partial(_trunk_kernel, sub_rows=sub_rows, tiles_per_seq=seq // tile,
                          n_prompt_tiles=n_prompt_tiles, sample_seq=sample_seq,
                          sample_pos0=PAST_LEN),
        grid=(n_prompt_tiles + n_sample_tiles,),
        in_specs=[pl.BlockSpec((tile, D_MODEL), prompt_tile),
                  pl.BlockSpec((tile, PLE_DIM), prompt_tile),
                  pl.BlockSpec((sub_rows, D_MODEL), sample_tile),
                  pl.BlockSpec((sub_rows, PLE_DIM), sample_tile),
                  pl.BlockSpec((per_tile, HIST_C, W_A), sample_streams),
                  pl.BlockSpec((per_tile, HIST_P, W_B), sample_streams)] + _weight_specs(),
        out_specs=[pl.BlockSpec((tile, D_MODEL), prompt_tile),
                   pl.BlockSpec((sub_rows, D_MODEL), sample_tile),
                   pl.BlockSpec((batch, HIST_C, W_A), whole),
                   pl.BlockSpec((batch, HIST_P, W_B), whole),
                   pl.BlockSpec((per_tile, HIST_C, W_A), sample_streams),
                   pl.BlockSpec((per_tile, HIST_P, W_B), sample_streams)],
        out_shape=[
            jax.ShapeDtypeStruct((batch * seq, D_MODEL), f32),
            jax.ShapeDtypeStruct((streams * sample_seq, D_MODEL), f32),
            jax.ShapeDtypeStruct((batch, HIST_C, W_A), f32),
            jax.ShapeDtypeStruct((batch, HIST_P, W_B), f32),
            jax.ShapeDtypeStruct((streams, HIST_C, W_A), f32),
            jax.ShapeDtypeStruct((streams, HIST_P, W_B), f32),
        ],
        scratch_shapes=[pltpu.VMEM((HIST_C_PAD, W_A), f32), pltpu.VMEM((HIST_P_PAD, W_B), f32)],
        compiler_params=pltpu.CompilerParams(
            dimension_semantics=("arbitrary",),
            vmem_limit_bytes=VMEM_LIMIT_BYTES),
        name="trunk",
    )(x.reshape(batch * seq, D_MODEL), p.reshape(batch * seq, PLE_DIM),
      xs.reshape(streams * sample_seq, D_MODEL), ps.reshape(streams * sample_seq, PLE_DIM),
      cache_conv, cache_pool, *weights)
    return (y.reshape(batch, seq, D_MODEL), ys.reshape(streams, sample_seq, D_MODEL),
            conv_p, pool_p, conv_s, pool_s)


def _prepare_weights(g_mix, w_in, conv_w, conv_b, pool_w, pool_scale, w_out, g_ple, w_ple_gate,
                     w_ple, g_final):
    bf16 = jnp.bfloat16
    zero = jnp.zeros((GC, GC), pool_w.dtype)
    pairs = [jnp.block([[pool_w[2 * i], zero], [zero, pool_w[2 * i + 1]]])
             for i in range(len(POOL_WINDOWS) // 2)]
    return (g_mix.reshape(1, D_MODEL), w_in.astype(bf16), conv_w, conv_b.reshape(1, W_A),
            jnp.stack(pairs).astype(bf16), pool_scale.reshape(1, W_B), w_out.astype(bf16),
            g_ple.reshape(1, D_MODEL), w_ple_gate.astype(bf16), w_ple.astype(bf16),
            g_final.reshape(1, D_MODEL))


def kernel(x_prompt, x_sample, cache_conv, cache_pool, p_prompt, p_sample, g_mix, w_in, conv_w,
           conv_b, pool_w, pool_scale, w_out, g_ple, w_ple_gate, w_ple, g_final):
    depth = g_mix.shape[0]
    assert depth == 1, "single-layer trunk"
    weights = _prepare_weights(g_mix[0], w_in[0], conv_w[0], conv_b[0], pool_w[0], pool_scale[0],
                               w_out[0], g_ple[0], w_ple_gate[0], w_ple[0], g_final)
    y_prompt, y_sample, conv_p, pool_p, conv_s, pool_s = _run_trunk(
        x_prompt, p_prompt[0], x_sample, p_sample[0], cache_conv[0], cache_pool[0], weights)
    return y_prompt, y_sample, conv_p[None], pool_p[None], conv_s[None], pool_s[None]
```

```python
import functools

import jax
import jax.numpy as jnp
from jax.experimental import pallas as pl
from jax.experimental.pallas import tpu as pltpu

D_MODEL = 1024
W_A = 512
W_B = 512
CONV_W = 3
HIST_C = CONV_W - 1
POOL_WINDOWS = (2, 4, 8, 16)
GC = W_B // len(POOL_WINDOWS)
HIST_P = max(POOL_WINDOWS) - 1
PLE_DIM = 256
PROJ_W = 4 * W_A + 2 * W_B
EPS = 1e-6
PAST_LEN = 2048
LOG2_E = 1.4426950408889634

SUBLANES = 8
HIST_C_PAD = SUBLANES
HIST_P_PAD = 2 * SUBLANES
PROMPT_TILE = 1024
PROMPT_SUB_ROWS = 256
VMEM_LIMIT_BYTES = 56 * 1024 * 1024


def _rmsnorm(x, g):
    ms = jnp.mean(x * x, axis=-1, keepdims=True)
    return x * jax.lax.rsqrt(ms + EPS) * g


def _sigmoid(z):
    return 1.0 / (1.0 + jnp.exp2(z * -LOG2_E))


def _silu(z):
    return z / (1.0 + jnp.exp2(z * -LOG2_E))


def _dot(a, w):
    return jnp.dot(a.astype(jnp.bfloat16), w, preferred_element_type=jnp.float32)


def _shift_rows(ext, k, pad):
    return pltpu.roll(ext, k, axis=0)[pad:]


def _conv_mixer(hA, bA, cA, zA, u_hist, conv_w, conv_b):
    u = cA * hA
    ext = jnp.concatenate([u_hist, u], axis=0)
    taps = [conv_w[:, k * W_A:(k + 1) * W_A] for k in range(CONV_W)]
    c = (_shift_rows(ext, 2, HIST_C_PAD) * taps[0] + _shift_rows(ext, 1, HIST_C_PAD) * taps[1]
         + u * taps[2] + conv_b)
    return bA * c * _silu(zA), u


def _pool_group(ext, v, window, pos0):
    rows = v.shape[0]
    total = ext.shape[0]
    s = ext
    span = 1
    while span < window:
        s = s + jnp.concatenate([s[total - span:], s[:total - span]], axis=0)
        span *= 2
    s = s[HIST_P_PAD:]
    head = HIST_P_PAD
    r = jax.lax.broadcasted_iota(jnp.int32, (head, GC), 0)
    cnt = jnp.minimum(pos0 + r + 1, window).astype(jnp.float32)
    mean_head = s[:head] / cnt
    if rows > head:
        mean = jnp.concatenate([mean_head, s[head:] * (1.0 / window)], axis=0)
    else:
        mean = mean_head
    return mean - v


def _pooled(vB, v_hist, pos0):
    pooled = []
    for g, window in enumerate(POOL_WINDOWS):
        sl = slice(g * GC, (g + 1) * GC)
        ext = jnp.concatenate([v_hist[:, sl], vB[:, sl]], axis=0)
        pooled.append(_pool_group(ext, vB[:, sl], window, pos0))
    return jnp.concatenate(pooled, axis=-1)


def _split_proj(proj):
    hA = proj[:, 0 * W_A:1 * W_A]
    bA = proj[:, 1 * W_A:2 * W_A]
    cA = proj[:, 2 * W_A:3 * W_A]
    zA = proj[:, 3 * W_A:4 * W_A]
    vB = proj[:, 4 * W_A:4 * W_A + W_B]
    zB = proj[:, 4 * W_A + W_B:]
    return hA, bA, cA, zA, vB, zB


class _Weights:
    def __init__(self, refs):
        (self.g_mix, self.w_in, self.conv_w, self.conv_b, self.pool_w, self.pool_scale,
         self.w_out, self.g_ple, self.w_gate, self.w_ple, self.g_final) = [r[...] for r in refs]


def _segment_stages(w, x, p, hists, pos0, carry, emit):
    s = {}

    def norm_in():
        s["hn"] = _rmsnorm(x, w.g_mix).astype(jnp.bfloat16)

    def project():
        s["proj"] = _dot(s.pop("hn"), w.w_in)

    def mix():
        hA, bA, cA, zA, vB, zB = _split_proj(s.pop("proj"))
        stream_hists = hists()
        seq = x.shape[0] // len(stream_hists)
        yA, pooled, tails = [], [], []
        for i, (u_hist, v_hist) in enumerate(stream_hists):
            sl = slice(i * seq, (i + 1) * seq)
            yA_i, u = _conv_mixer(hA[sl], bA[sl], cA[sl], zA[sl], u_hist, w.conv_w, w.conv_b)
            yA.append(yA_i)
            pooled.append(_pooled(vB[sl], v_hist, pos0))
            tails.append((u[seq - HIST_C_PAD:], vB[sl][seq - HIST_P_PAD:]))
        carry(tails)
        s["yA"] = jnp.concatenate(yA, axis=0).astype(jnp.bfloat16)
        s["pooled"] = jnp.concatenate(pooled, axis=0).astype(jnp.bfloat16)
        s["gB"] = _silu(zB)

    def project_pool():
        pooled = s.pop("pooled")
        s["mixed"] = [_dot(pooled[:, 2 * GC * i:2 * GC * (i + 1)], w.pool_w[i])
                      for i in range(len(POOL_WINDOWS) // 2)]

    def gate_pool():
        yB = jnp.concatenate(s.pop("mixed"), axis=-1) * w.pool_scale * s.pop("gB")
        s["y_mix"] = jnp.concatenate([s.pop("yA"), yB.astype(jnp.bfloat16)], axis=-1)

    def project_out():
        s["o"] = _dot(s.pop("y_mix"), w.w_out)

    def norm_gate():
        s["h"] = x + s.pop("o")
        s["hn2"] = _rmsnorm(s["h"], w.g_ple).astype(jnp.bfloat16)

    def project_gate():
        s["z"] = _dot(s.pop("hn2"), w.w_gate)
        s["e"] = _dot(p, w.w_ple)

    def finish():
        h = s.pop("h") + s.pop("e") * _sigmoid(s.pop("z"))
        emit(_rmsnorm(h, w.g_final))

    return [norm_in, project, mix, project_pool, gate_pool, project_out, norm_gate,
            project_gate, finish]


def _run_skewed(stage_lists, skew):
    depth = len(stage_lists[0])
    for slot in range(skew * (len(stage_lists) - 1) + depth):
        for k, stages in enumerate(stage_lists):
            j = slot - skew * k
            if 0 <= j < depth:
                stages[j]()


N_WEIGHTS = 11


def _trunk_kernel(x_ref, p_ref, xs_ref, ps_ref, cache_conv_ref, cache_pool_ref, *refs,
                  sub_rows, tiles_per_seq, n_prompt_tiles, sample_seq, sample_pos0):
    weight_refs = refs[:N_WEIGHTS]
    (y_ref, ys_ref, conv_p_ref, pool_p_ref, conv_s_ref, pool_s_ref,
     u_hist_ref, v_hist_ref) = refs[N_WEIGHTS:]
    t = pl.program_id(0)
    tile_in_seq = jax.lax.rem(t, tiles_per_seq)

    @pl.when(jnp.logical_and(t < n_prompt_tiles, tile_in_seq == 0))
    def _():
        u_hist_ref[...] = jnp.zeros_like(u_hist_ref)
        v_hist_ref[...] = jnp.zeros_like(v_hist_ref)

    @pl.when(t < n_prompt_tiles)
    def _():
        rows = x_ref.shape[0]
        w = _Weights(weight_refs)
        tails = [(u_hist_ref[...], v_hist_ref[...])]
        stage_lists = []
        for k in range(rows // sub_rows):
            sl = pl.ds(k * sub_rows, sub_rows)

            def emit(out, sl=sl):
                y_ref[sl, :] = out

            stage_lists.append(_segment_stages(
                w, x_ref[sl, :], p_ref[sl, :], lambda k=k: [tails[k]],
                tile_in_seq * rows + k * sub_rows, lambda new: tails.append(new[0]), emit))
        _run_skewed(stage_lists, skew=2)
        u_tail, v_tail = tails[-1]
        u_hist_ref[...] = u_tail
        v_hist_ref[...] = v_tail
        seq = jax.lax.div(t, tiles_per_seq)
        conv_p_ref[seq] = u_tail[HIST_C_PAD - HIST_C:]
        for b in range(pool_p_ref.shape[1]):
            @pl.when(seq == b)
            def _(b=b):
                pool_p_ref[:, b, :] = v_tail[HIST_P_PAD - HIST_P:]

    @pl.when(t >= n_prompt_tiles)
    def _():
        streams = xs_ref.shape[0] // sample_seq

        def hists():
            zeros_c = jnp.zeros((HIST_C_PAD - HIST_C, W_A), jnp.float32)
            zeros_p = jnp.zeros((HIST_P_PAD - HIST_P, W_B), jnp.float32)
            return [(jnp.concatenate([zeros_c, cache_conv_ref[i]], axis=0),
                     jnp.concatenate([zeros_p, cache_pool_ref[:, i, :]], axis=0))
                    for i in range(streams)]

        def carry(tails):
            for i, (u_tail, v_tail) in enumerate(tails):
                conv_s_ref[i] = u_tail[HIST_C_PAD - HIST_C:]
                pool_s_ref[:, i, :] = v_tail[HIST_P_PAD - HIST_P:]

        def emit(out):
            ys_ref[...] = out

        stages = _segment_stages(_Weights(weight_refs), xs_ref[...], ps_ref[...], hists,
                                 sample_pos0, carry, emit)
        _run_skewed([stages], skew=1)


def _resident(shape):
    zeros = (0,) * len(shape)
    return pl.BlockSpec(shape, lambda *_: zeros, pipeline_mode=pl.Buffered(1))


def _weight_specs():
    specs = [
        _resident((1, D_MODEL)),
        _resident((D_MODEL, PROJ_W)),
        _resident((1, CONV_W * W_A)),
        _resident((1, W_A)),
        _resident((2, 2 * GC, 2 * GC)),
        _resident((1, W_B)),
        _resident((D_MODEL, D_MODEL)),
        _resident((1, D_MODEL)),
        _resident((D_MODEL, D_MODEL)),
        _resident((PLE_DIM, D_MODEL)),
        _resident((1, D_MODEL)),
    ]
    assert len(specs) == N_WEIGHTS
    return specs


def _run_trunk(x, p, xs, ps, cache_conv, cache_pool, weights):
    batch, seq, _ = x.shape
    streams, sample_seq, _ = xs.shape
    tile, sub_rows = PROMPT_TILE, PROMPT_SUB_ROWS
    assert seq % tile == 0 and tile % sub_rows == 0 and sub_rows >= HIST_P_PAD
    assert sub_rows % sample_seq == 0 and sample_seq >= HIST_P_PAD and sample_seq % SUBLANES == 0
    per_tile = sub_rows // sample_seq
    assert streams % per_tile == 0
    n_prompt_tiles = batch * seq // tile
    n_sample_tiles = streams // per_tile
    prompt_tile = lambda t: (jnp.minimum(t, n_prompt_tiles - 1), 0)
    sample_tile = lambda t: (jnp.clip(t - n_prompt_tiles, 0, n_sample_tiles - 1), 0)
    sample_streams = lambda t: sample_tile(t) + (0,)
    sample_rows_streams = lambda t: (0,) + sample_tile(t)
    whole = lambda t: (0, 0, 0)
    f32 = jnp.float32
    y, ys, conv_p, pool_p, conv_s, pool_s = pl.pallas_call(
        functools.partial(_trunk_kernel, sub_rows=sub_rows, tiles_per_seq=seq // tile,
                          n_prompt_tiles=n_prompt_tiles, sample_seq=sample_seq,
                          sample_pos0=PAST_LEN),
        grid=(n_prompt_tiles + n_sample_tiles,),
        in_specs=[pl.BlockSpec((tile, D_MODEL), prompt_tile),
                  pl.BlockSpec((tile, PLE_DIM), prompt_tile),
                  pl.BlockSpec((sub_rows, D_MODEL), sample_tile),
                  pl.BlockSpec((sub_rows, PLE_DIM), sample_tile),
                  pl.BlockSpec((per_tile, HIST_C, W_A), sample_streams),
                  pl.BlockSpec((HIST_P, per_tile, W_B), sample_rows_streams)] + _weight_specs(),
        out_specs=[pl.BlockSpec((tile, D_MODEL), prompt_tile),
                   pl.BlockSpec((sub_rows, D_MODEL), sample_tile),
                   pl.BlockSpec((batch, HIST_C, W_A), whole),
                   pl.BlockSpec((HIST_P, batch, W_B), whole),
                   pl.BlockSpec((per_tile, HIST_C, W_A), sample_streams),
                   pl.BlockSpec((HIST_P, per_tile, W_B), sample_rows_streams)],
        out_shape=[
            jax.ShapeDtypeStruct((batch * seq, D_MODEL), f32),
            jax.ShapeDtypeStruct((streams * sample_seq, D_MODEL), f32),
            jax.ShapeDtypeStruct((batch, HIST_C, W_A), f32),
            jax.ShapeDtypeStruct((HIST_P, batch, W_B), f32),
            jax.ShapeDtypeStruct((streams, HIST_C, W_A), f32),
            jax.ShapeDtypeStruct((HIST_P, streams, W_B), f32),
        ],
        scratch_shapes=[pltpu.VMEM((HIST_C_PAD, W_A), f32), pltpu.VMEM((HIST_P_PAD, W_B), f32)],
        compiler_params=pltpu.CompilerParams(
            dimension_semantics=("arbitrary",),
            vmem_limit_bytes=VMEM_LIMIT_BYTES),
        name="trunk",
    )(x.reshape(batch * seq, D_MODEL), p.reshape(batch * seq, PLE_DIM),
      xs.reshape(streams * sample_seq, D_MODEL), ps.reshape(streams * sample_seq, PLE_DIM),
      cache_conv, cache_pool.transpose(1, 0, 2), *weights)
    return (y.reshape(batch, seq, D_MODEL), ys.reshape(streams, sample_seq, D_MODEL),
            conv_p, pool_p.transpose(1, 0, 2), conv_s, pool_s.transpose(1, 0, 2))


def _prepare_weights(g_mix, w_in, conv_w, conv_b, pool_w, pool_scale, w_out, g_ple, w_ple_gate,
                     w_ple, g_final):
    bf16 = jnp.bfloat16
    zero = jnp.zeros((GC, GC), pool_w.dtype)
    pairs = [jnp.block([[pool_w[2 * i], zero], [zero, pool_w[2 * i + 1]]])
             for i in range(len(POOL_WINDOWS) // 2)]
    return (g_mix.reshape(1, D_MODEL), w_in.astype(bf16), conv_w.reshape(1, CONV_W * W_A),
            conv_b.reshape(1, W_A),
            jnp.stack(pairs).astype(bf16), pool_scale.reshape(1, W_B), w_out.astype(bf16),
            g_ple.reshape(1, D_MODEL), w_ple_gate.astype(bf16), w_ple.astype(bf16),
            g_final.reshape(1, D_MODEL))


def kernel(x_prompt, x_sample, cache_conv, cache_pool, p_prompt, p_sample, g_mix, w_in, conv_w,
           conv_b, pool_w, pool_scale, w_out, g_ple, w_ple_gate, w_ple, g_final):
    depth = g_mix.shape[0]
    assert depth == 1, "single-layer trunk"
    weights = _prepare_weights(g_mix[0], w_in[0], conv_w[0], conv_b[0], pool_w[0], pool_scale[0],
                               w_out[0], g_ple[0], w_ple_gate[0], w_ple[0], g_final)
    y_prompt, y_sample, conv_p, pool_p, conv_s, pool_s = _run_trunk(
        x_prompt, p_prompt[0], x_sample, p_sample[0], cache_conv[0], cache_pool[0], weights)
    return y_prompt, y_sample, conv_p[None], pool_p[None], conv_s[None], pool_s[None]
```

```python
import functools

import jax
import jax.numpy as jnp
from jax.experimental import pallas as pl
from jax.experimental.pallas import tpu as pltpu

D_MODEL = 1024
W_A = 512
W_B = 512
CONV_W = 3
HIST_C = CONV_W - 1
POOL_WINDOWS = (2, 4, 8, 16)
GC = W_B // len(POOL_WINDOWS)
HIST_P = max(POOL_WINDOWS) - 1
PLE_DIM = 256
PROJ_W = 4 * W_A + 2 * W_B
EPS = 1e-6
PAST_LEN = 2048
LOG2_E = 1.4426950408889634

SUBLANES = 8
HIST_C_PAD = SUBLANES
HIST_P_PAD = 2 * SUBLANES
PROMPT_TILE = 1024
PROMPT_SUB_ROWS = 256
CAST_CHUNK_ROWS = 128
VMEM_LIMIT_BYTES = 56 * 1024 * 1024


def _rmsnorm(x, g):
    ms = jnp.mean(x * x, axis=-1, keepdims=True)
    return x * jax.lax.rsqrt(ms + EPS) * g


def _sigmoid(z):
    return 1.0 / (1.0 + jnp.exp2(z * -LOG2_E))


def _silu(z):
    return z / (1.0 + jnp.exp2(z * -LOG2_E))


def _dot(a, w):
    return jnp.dot(a.astype(jnp.bfloat16), w, preferred_element_type=jnp.float32)


def _shift_rows(ext, k, pad):
    return pltpu.roll(ext, k, axis=0)[pad:]


def _conv_mixer(hA, bA, cA, zA, u_hist, conv_w, conv_b):
    u = cA * hA
    ext = jnp.concatenate([u_hist, u], axis=0)
    taps = [conv_w[:, k * W_A:(k + 1) * W_A] for k in range(CONV_W)]
    c = (_shift_rows(ext, 2, HIST_C_PAD) * taps[0] + _shift_rows(ext, 1, HIST_C_PAD) * taps[1]
         + u * taps[2] + conv_b)
    return bA * c * _silu(zA), u


def _pool_group(ext, v, window, pos0):
    rows = v.shape[0]
    total = ext.shape[0]
    s = ext
    span = 1
    while span < window:
        s = s + jnp.concatenate([s[total - span:], s[:total - span]], axis=0)
        span *= 2
    s = s[HIST_P_PAD:]
    head = HIST_P_PAD
    r = jax.lax.broadcasted_iota(jnp.int32, (head, GC), 0)
    cnt = jnp.minimum(pos0 + r + 1, window).astype(jnp.float32)
    mean_head = s[:head] / cnt
    if rows > head:
        mean = jnp.concatenate([mean_head, s[head:] * (1.0 / window)], axis=0)
    else:
        mean = mean_head
    return mean - v


def _pooled(vB, v_hist, pos0):
    pooled = []
    for g, window in enumerate(POOL_WINDOWS):
        sl = slice(g * GC, (g + 1) * GC)
        ext = jnp.concatenate([v_hist[:, sl], vB[:, sl]], axis=0)
        pooled.append(_pool_group(ext, vB[:, sl], window, pos0))
    return jnp.concatenate(pooled, axis=-1)


def _split_proj(proj):
    hA = proj[:, 0 * W_A:1 * W_A]
    bA = proj[:, 1 * W_A:2 * W_A]
    cA = proj[:, 2 * W_A:3 * W_A]
    zA = proj[:, 3 * W_A:4 * W_A]
    vB = proj[:, 4 * W_A:4 * W_A + W_B]
    zB = proj[:, 4 * W_A + W_B:]
    return hA, bA, cA, zA, vB, zB


class _Weights:
    def __init__(self, refs):
        (self.g_mix, self.w_in, self.conv_w, self.conv_b, self.pool_w, self.pool_scale,
         self.w_out, self.g_ple, self.w_gate, self.w_ple, self.g_final) = [r[...] for r in refs]


def _segment_stages(w, x, p, hists, pos0, carry, emit):
    s = {}

    def norm_in():
        s["hn"] = _rmsnorm(x, w.g_mix).astype(jnp.bfloat16)

    def project():
        s["proj"] = _dot(s.pop("hn"), w.w_in)

    def mix():
        hA, bA, cA, zA, vB, zB = _split_proj(s.pop("proj"))
        stream_hists = hists()
        seq = x.shape[0] // len(stream_hists)
        yA, pooled, tails = [], [], []
        for i, (u_hist, v_hist) in enumerate(stream_hists):
            sl = slice(i * seq, (i + 1) * seq)
            yA_i, u = _conv_mixer(hA[sl], bA[sl], cA[sl], zA[sl], u_hist, w.conv_w, w.conv_b)
            yA.append(yA_i)
            pooled.append(_pooled(vB[sl], v_hist, pos0))
            tails.append((u[seq - HIST_C_PAD:], vB[sl][seq - HIST_P_PAD:]))
        carry(tails)
        s["yA"] = jnp.concatenate(yA, axis=0).astype(jnp.bfloat16)
        s["pooled"] = jnp.concatenate(pooled, axis=0).astype(jnp.bfloat16)
        s["gB"] = _silu(zB)

    def project_pool():
        pooled = s.pop("pooled")
        s["mixed"] = [_dot(pooled[:, 2 * GC * i:2 * GC * (i + 1)], w.pool_w[i])
                      for i in range(len(POOL_WINDOWS) // 2)]

    def gate_pool():
        yB = jnp.concatenate(s.pop("mixed"), axis=-1) * w.pool_scale * s.pop("gB")
        s["y_mix"] = jnp.concatenate([s.pop("yA"), yB.astype(jnp.bfloat16)], axis=-1)

    def project_out():
        s["o"] = _dot(s.pop("y_mix"), w.w_out)

    def norm_gate():
        s["h"] = x + s.pop("o")
        s["hn2"] = _rmsnorm(s["h"], w.g_ple).astype(jnp.bfloat16)

    def project_gate():
        s["z"] = _dot(s.pop("hn2"), w.w_gate)
        s["e"] = _dot(p, w.w_ple)

    def finish():
        h = s.pop("h") + s.pop("e") * _sigmoid(s.pop("z"))
        emit(_rmsnorm(h, w.g_final))

    return [norm_in, project, mix, project_pool, gate_pool, project_out, norm_gate,
            project_gate, finish]


def _run_skewed(stage_lists, skew):
    depth = len(stage_lists[0])
    for slot in range(skew * (len(stage_lists) - 1) + depth):
        for k, stages in enumerate(stage_lists):
            j = slot - skew * k
            if 0 <= j < depth:
                stages[j]()


def _cast_matmul_weights(hbm_refs, bf16_refs, stage_ref, pool_stage_ref, sems):
    w_in, pool_w, w_out, w_gate, w_ple = hbm_refs
    w_in_s, pool_w_s, w_out_s, w_gate_s, w_ple_s = bf16_refs
    pool_copy = pltpu.make_async_copy(pool_w, pool_stage_ref, sems.at[2])
    pool_copy.start()
    chunks = []
    for src, dst in ((w_in, w_in_s), (w_out, w_out_s), (w_gate, w_gate_s), (w_ple, w_ple_s)):
        n_rows, width = src.shape
        for r in range(0, n_rows, CAST_CHUNK_ROWS):
            rows = pl.ds(r, CAST_CHUNK_ROWS)
            chunks.append((src.at[rows], dst.at[rows], width))

    def copy(i):
        src, _, width = chunks[i]
        return pltpu.make_async_copy(src, stage_ref.at[i % 2, :, pl.ds(0, width)], sems.at[i % 2])

    copy(0).start()
    for i, (_, dst, width) in enumerate(chunks):
        if i + 1 < len(chunks):
            copy(i + 1).start()
        copy(i).wait()
        dst[...] = stage_ref[i % 2, :, :width].astype(jnp.bfloat16)
    pool_copy.wait()
    pool_w_s[...] = jnp.zeros_like(pool_w_s)
    for g in range(len(POOL_WINDOWS)):
        pair, half = divmod(g, 2)
        block = slice(half * GC, (half + 1) * GC)
        pool_w_s[pair, block, block] = pool_stage_ref[g].astype(jnp.bfloat16)


def _trunk_kernel(x_ref, p_ref, xs_ref, ps_ref, cache_conv_ref, cache_pool_ref, *refs,
                  sub_rows, tiles_per_seq, n_prompt_tiles, sample_seq, sample_pos0):
    g_mix_ref, conv_w_ref, conv_b_ref, pool_scale_ref, g_ple_ref, g_final_ref = refs[:6]
    hbm_weight_refs = refs[6:11]
    (y_ref, ys_ref, conv_p_ref, pool_p_ref, conv_s_ref, pool_s_ref,
     u_hist_ref, v_hist_ref) = refs[11:19]
    bf16_weight_refs = refs[19:24]
    stage_ref, pool_stage_ref, sems = refs[24:]
    w_in_ref, pool_w_ref, w_out_ref, w_gate_ref, w_ple_ref = bf16_weight_refs
    weight_refs = (g_mix_ref, w_in_ref, conv_w_ref, conv_b_ref, pool_w_ref, pool_scale_ref,
                   w_out_ref, g_ple_ref, w_gate_ref, w_ple_ref, g_final_ref)
    t = pl.program_id(0)
    tile_in_seq = jax.lax.rem(t, tiles_per_seq)

    @pl.when(t == 0)
    def _():
        _cast_matmul_weights(hbm_weight_refs, bf16_weight_refs, stage_ref, pool_stage_ref, sems)

    @pl.when(jnp.logical_and(t < n_prompt_tiles, tile_in_seq == 0))
    def _():
        u_hist_ref[...] = jnp.zeros_like(u_hist_ref)
        v_hist_ref[...] = jnp.zeros_like(v_hist_ref)

    @pl.when(t < n_prompt_tiles)
    def _():
        rows = x_ref.shape[0]
        w = _Weights(weight_refs)
        tails = [(u_hist_ref[...], v_hist_ref[...])]
        stage_lists = []
        for k in range(rows // sub_rows):
            sl = pl.ds(k * sub_rows, sub_rows)

            def emit(out, sl=sl):
                y_ref[sl, :] = out

            stage_lists.append(_segment_stages(
                w, x_ref[sl, :], p_ref[sl, :], lambda k=k: [tails[k]],
                tile_in_seq * rows + k * sub_rows, lambda new: tails.append(new[0]), emit))
        _run_skewed(stage_lists, skew=2)
        u_tail, v_tail = tails[-1]
        u_hist_ref[...] = u_tail
        v_hist_ref[...] = v_tail
        seq = jax.lax.div(t, tiles_per_seq)
        conv_p_ref[seq] = u_tail[HIST_C_PAD - HIST_C:]
        for b in range(pool_p_ref.shape[1]):
            @pl.when(seq == b)
            def _(b=b):
                pool_p_ref[:, b, :] = v_tail[HIST_P_PAD - HIST_P:]

    @pl.when(t >= n_prompt_tiles)
    def _():
        streams = xs_ref.shape[0] // sample_seq

        def hists():
            zeros_c = jnp.zeros((HIST_C_PAD - HIST_C, W_A), jnp.float32)
            zeros_p = jnp.zeros((HIST_P_PAD - HIST_P, W_B), jnp.float32)
            return [(jnp.concatenate([zeros_c, cache_conv_ref[i]], axis=0),
                     jnp.concatenate([zeros_p, cache_pool_ref[:, i, :]], axis=0))
                    for i in range(streams)]

        def carry(tails):
            for i, (u_tail, v_tail) in enumerate(tails):
                conv_s_ref[i] = u_tail[HIST_C_PAD - HIST_C:]
                pool_s_ref[:, i, :] = v_tail[HIST_P_PAD - HIST_P:]

        def emit(out):
            ys_ref[...] = out

        stages = _segment_stages(_Weights(weight_refs), xs_ref[...], ps_ref[...], hists,
                                 sample_pos0, carry, emit)
        _run_skewed([stages], skew=1)


def _resident(shape):
    zeros = (0,) * len(shape)
    return pl.BlockSpec(shape, lambda *_: zeros, pipeline_mode=pl.Buffered(1))


def _small_param_specs():
    return [_resident((1, D_MODEL)), _resident((1, CONV_W * W_A)), _resident((1, W_A)),
            _resident((1, W_B)), _resident((1, D_MODEL)), _resident((1, D_MODEL))]


def _run_trunk(x, p, xs, ps, cache_conv, cache_pool, small_params, matmul_weights):
    batch, seq, _ = x.shape
    streams, sample_seq, _ = xs.shape
    tile, sub_rows = PROMPT_TILE, PROMPT_SUB_ROWS
    assert seq % tile == 0 and tile % sub_rows == 0 and sub_rows >= HIST_P_PAD
    assert sub_rows % sample_seq == 0 and sample_seq >= HIST_P_PAD and sample_seq % SUBLANES == 0
    per_tile = sub_rows // sample_seq
    assert streams % per_tile == 0
    n_prompt_tiles = batch * seq // tile
    n_sample_tiles = streams // per_tile
    prompt_tile = lambda t: (jnp.minimum(t, n_prompt_tiles - 1), 0)
    sample_tile = lambda t: (jnp.clip(t - n_prompt_tiles, 0, n_sample_tiles - 1), 0)
    sample_streams = lambda t: sample_tile(t) + (0,)
    sample_rows_streams = lambda t: (0,) + sample_tile(t)
    whole = lambda t: (0, 0, 0)
    f32, bf16 = jnp.float32, jnp.bfloat16
    y, ys, conv_p, pool_p, conv_s, pool_s = pl.pallas_call(
        functools.partial(_trunk_kernel, sub_rows=sub_rows, tiles_per_seq=seq // tile,
                          n_prompt_tiles=n_prompt_tiles, sample_seq=sample_seq,
                          sample_pos0=PAST_LEN),
        grid=(n_prompt_tiles + n_sample_tiles,),
        in_specs=[pl.BlockSpec((tile, D_MODEL), prompt_tile),
                  pl.BlockSpec((tile, PLE_DIM), prompt_tile),
                  pl.BlockSpec((sub_rows, D_MODEL), sample_tile),
                  pl.BlockSpec((sub_rows, PLE_DIM), sample_tile),
                  pl.BlockSpec((per_tile, HIST_C, W_A), sample_streams),
                  pl.BlockSpec((HIST_P, per_tile, W_B), sample_rows_streams)]
        + _small_param_specs() + [pl.BlockSpec(memory_space=pl.ANY)] * len(matmul_weights),
        out_specs=[pl.BlockSpec((tile, D_MODEL), prompt_tile),
                   pl.BlockSpec((sub_rows, D_MODEL), sample_tile),
                   pl.BlockSpec((batch, HIST_C, W_A), whole),
                   pl.BlockSpec((HIST_P, batch, W_B), whole),
                   pl.BlockSpec((per_tile, HIST_C, W_A), sample_streams),
                   pl.BlockSpec((HIST_P, per_tile, W_B), sample_rows_streams)],
        out_shape=[
            jax.ShapeDtypeStruct((batch * seq, D_MODEL), f32),
            jax.ShapeDtypeStruct((streams * sample_seq, D_MODEL), f32),
            jax.ShapeDtypeStruct((batch, HIST_C, W_A), f32),
            jax.ShapeDtypeStruct((HIST_P, batch, W_B), f32),
            jax.ShapeDtypeStruct((streams, HIST_C, W_A), f32),
            jax.ShapeDtypeStruct((HIST_P, streams, W_B), f32),
        ],
        scratch_shapes=[
            pltpu.VMEM((HIST_C_PAD, W_A), f32), pltpu.VMEM((HIST_P_PAD, W_B), f32),
            pltpu.VMEM((D_MODEL, PROJ_W), bf16),
            pltpu.VMEM((2, 2 * GC, 2 * GC), bf16),
            pltpu.VMEM((D_MODEL, D_MODEL), bf16),
            pltpu.VMEM((D_MODEL, D_MODEL), bf16),
            pltpu.VMEM((PLE_DIM, D_MODEL), bf16),
            pltpu.VMEM((2, CAST_CHUNK_ROWS, PROJ_W), f32),
            pltpu.VMEM((len(POOL_WINDOWS), GC, GC), f32),
            pltpu.SemaphoreType.DMA((3,)),
        ],
        compiler_params=pltpu.CompilerParams(
            dimension_semantics=("arbitrary",),
            vmem_limit_bytes=VMEM_LIMIT_BYTES),
        name="trunk",
    )(x.reshape(batch * seq, D_MODEL), p.reshape(batch * seq, PLE_DIM),
      xs.reshape(streams * sample_seq, D_MODEL), ps.reshape(streams * sample_seq, PLE_DIM),
      cache_conv, cache_pool.transpose(1, 0, 2), *small_params, *matmul_weights)
    return (y.reshape(batch, seq, D_MODEL), ys.reshape(streams, sample_seq, D_MODEL),
            conv_p, pool_p.transpose(1, 0, 2), conv_s, pool_s.transpose(1, 0, 2))


def kernel(x_prompt, x_sample, cache_conv, cache_pool, p_prompt, p_sample, g_mix, w_in, conv_w,
           conv_b, pool_w, pool_scale, w_out, g_ple, w_ple_gate, w_ple, g_final):
    depth = g_mix.shape[0]
    assert depth == 1, "single-layer trunk"
    small_params = (g_mix, conv_w.reshape(1, CONV_W * W_A), conv_b, pool_scale, g_ple,
                    g_final.reshape(1, D_MODEL))
    matmul_weights = (w_in[0], pool_w[0], w_out[0], w_ple_gate[0], w_ple[0])
    y_prompt, y_sample, conv_p, pool_p, conv_s, pool_s = _run_trunk(
        x_prompt, p_prompt[0], x_sample, p_sample[0], cache_conv[0], cache_pool[0],
        small_params, matmul_weights)
    return y_prompt, y_sample, conv_p[None], pool_p[None], conv_s[None], pool_s[None]
```

```python
import functools

import jax
import jax.numpy as jnp
from jax.experimental import pallas as pl
from jax.experimental.pallas import tpu as pltpu

D_MODEL = 1024
W_A = 512
W_B = 512
CONV_W = 3
HIST_C = CONV_W - 1
POOL_WINDOWS = (2, 4, 8, 16)
GC = W_B // len(POOL_WINDOWS)
HIST_P = max(POOL_WINDOWS) - 1
PLE_DIM = 256
PROJ_W = 4 * W_A + 2 * W_B
EPS = 1e-6
PAST_LEN = 2048
LOG2_E = 1.4426950408889634

SUBLANES = 8
HIST_C_PAD = SUBLANES
HIST_P_PAD = 2 * SUBLANES
PROMPT_TILE = 1024
PROMPT_SUB_ROWS = 256
CAST_CHUNK_ROWS = 128
CAST_BUFFERS = 3
VMEM_LIMIT_BYTES = 56 * 1024 * 1024


def _rmsnorm(x, g):
    ms = jnp.mean(x * x, axis=-1, keepdims=True)
    return x * jax.lax.rsqrt(ms + EPS) * g


def _sigmoid(z):
    return 1.0 / (1.0 + jnp.exp2(z * -LOG2_E))


def _silu(z):
    return z / (1.0 + jnp.exp2(z * -LOG2_E))


def _dot(a, w):
    return jnp.dot(a.astype(jnp.bfloat16), w, preferred_element_type=jnp.float32)


def _shift_rows(ext, k, pad):
    return pltpu.roll(ext, k, axis=0)[pad:]


def _conv_mixer(hA, bA, cA, zA, u_hist, conv_w, conv_b):
    u = cA * hA
    ext = jnp.concatenate([u_hist, u], axis=0)
    taps = [conv_w[:, k * W_A:(k + 1) * W_A] for k in range(CONV_W)]
    c = (_shift_rows(ext, 2, HIST_C_PAD) * taps[0] + _shift_rows(ext, 1, HIST_C_PAD) * taps[1]
         + u * taps[2] + conv_b)
    return bA * c * _silu(zA), u


def _pool_group(ext, v, window, pos0):
    rows = v.shape[0]
    total = ext.shape[0]
    s = ext
    span = 1
    while span < window:
        s = s + jnp.concatenate([s[total - span:], s[:total - span]], axis=0)
        span *= 2
    s = s[HIST_P_PAD:]
    head = HIST_P_PAD
    r = jax.lax.broadcasted_iota(jnp.int32, (head, GC), 0)
    cnt = jnp.minimum(pos0 + r + 1, window).astype(jnp.float32)
    mean_head = s[:head] / cnt
    if rows > head:
        mean = jnp.concatenate([mean_head, s[head:] * (1.0 / window)], axis=0)
    else:
        mean = mean_head
    return mean - v


def _pooled(vB, v_hist, pos0):
    pooled = []
    for g, window in enumerate(POOL_WINDOWS):
        sl = slice(g * GC, (g + 1) * GC)
        ext = jnp.concatenate([v_hist[:, sl], vB[:, sl]], axis=0)
        pooled.append(_pool_group(ext, vB[:, sl], window, pos0))
    return jnp.concatenate(pooled, axis=-1)


def _split_proj(proj):
    hA = proj[:, 0 * W_A:1 * W_A]
    bA = proj[:, 1 * W_A:2 * W_A]
    cA = proj[:, 2 * W_A:3 * W_A]
    zA = proj[:, 3 * W_A:4 * W_A]
    vB = proj[:, 4 * W_A:4 * W_A + W_B]
    zB = proj[:, 4 * W_A + W_B:]
    return hA, bA, cA, zA, vB, zB


class _Weights:
    def __init__(self, refs):
        (self.g_mix, self.w_in, self.conv_w, self.conv_b, self.pool_w, self.pool_scale,
         self.w_out, self.g_ple, self.w_gate, self.w_ple, self.g_final) = [r[...] for r in refs]


def _segment_stages(w, x, p, hists, pos0, carry, emit):
    s = {}

    def norm_in():
        s["hn"] = _rmsnorm(x, w.g_mix).astype(jnp.bfloat16)

    def project():
        s["proj"] = _dot(s.pop("hn"), w.w_in)

    def mix():
        hA, bA, cA, zA, vB, zB = _split_proj(s.pop("proj"))
        stream_hists = hists()
        seq = x.shape[0] // len(stream_hists)
        yA, pooled, tails = [], [], []
        for i, (u_hist, v_hist) in enumerate(stream_hists):
            sl = slice(i * seq, (i + 1) * seq)
            yA_i, u = _conv_mixer(hA[sl], bA[sl], cA[sl], zA[sl], u_hist, w.conv_w, w.conv_b)
            yA.append(yA_i)
            pooled.append(_pooled(vB[sl], v_hist, pos0))
            tails.append((u[seq - HIST_C_PAD:], vB[sl][seq - HIST_P_PAD:]))
        carry(tails)
        s["yA"] = jnp.concatenate(yA, axis=0).astype(jnp.bfloat16)
        s["pooled"] = jnp.concatenate(pooled, axis=0).astype(jnp.bfloat16)
        s["gB"] = _silu(zB)

    def project_pool():
        pooled = s.pop("pooled")
        s["mixed"] = [_dot(pooled[:, 2 * GC * i:2 * GC * (i + 1)], w.pool_w[i])
                      for i in range(len(POOL_WINDOWS) // 2)]

    def gate_pool():
        yB = jnp.concatenate(s.pop("mixed"), axis=-1) * w.pool_scale * s.pop("gB")
        s["y_mix"] = jnp.concatenate([s.pop("yA"), yB.astype(jnp.bfloat16)], axis=-1)

    def project_out():
        s["o"] = _dot(s.pop("y_mix"), w.w_out)

    def norm_gate():
        s["h"] = x + s.pop("o")
        s["hn2"] = _rmsnorm(s["h"], w.g_ple).astype(jnp.bfloat16)

    def project_gate():
        s["z"] = _dot(s.pop("hn2"), w.w_gate)
        s["e"] = _dot(p, w.w_ple)

    def finish():
        h = s.pop("h") + s.pop("e") * _sigmoid(s.pop("z"))
        emit(_rmsnorm(h, w.g_final))

    return [norm_in, project, mix, project_pool, gate_pool, project_out, norm_gate,
            project_gate, finish]


def _run_skewed(stage_lists, skew):
    depth = len(stage_lists[0])
    for slot in range(skew * (len(stage_lists) - 1) + depth):
        for k, stages in enumerate(stage_lists):
            j = slot - skew * k
            if 0 <= j < depth:
                stages[j]()


def _cast_matmul_weights(hbm_refs, bf16_refs, stage_ref, pool_stage_ref, sems):
    w_in, pool_w, w_out, w_gate, w_ple = hbm_refs
    w_in_s, pool_w_s, w_out_s, w_gate_s, w_ple_s = bf16_refs
    n_buf = stage_ref.shape[0]
    pool_copy = pltpu.make_async_copy(pool_w, pool_stage_ref, sems.at[n_buf])
    pool_copy.start()
    chunks = []
    for src, dst in ((w_in, w_in_s), (w_out, w_out_s), (w_gate, w_gate_s), (w_ple, w_ple_s)):
        n_rows, width = src.shape
        for r in range(0, n_rows, CAST_CHUNK_ROWS):
            rows = pl.ds(r, CAST_CHUNK_ROWS)
            chunks.append((src.at[rows], dst.at[rows], width))

    def copy(i):
        src, _, width = chunks[i]
        slot = i % n_buf
        return pltpu.make_async_copy(src, stage_ref.at[slot, :, pl.ds(0, width)], sems.at[slot])

    for i in range(min(n_buf - 1, len(chunks))):
        copy(i).start()
    for i, (_, dst, width) in enumerate(chunks):
        if i + n_buf - 1 < len(chunks):
            copy(i + n_buf - 1).start()
        copy(i).wait()
        dst[...] = stage_ref[i % n_buf, :, :width].astype(jnp.bfloat16)
    pool_copy.wait()
    pool_w_s[...] = jnp.zeros_like(pool_w_s)
    for g in range(len(POOL_WINDOWS)):
        pair, half = divmod(g, 2)
        block = slice(half * GC, (half + 1) * GC)
        pool_w_s[pair, block, block] = pool_stage_ref[g].astype(jnp.bfloat16)


def _trunk_kernel(x_ref, p_ref, xs_ref, ps_ref, cache_conv_ref, cache_pool_ref, *refs,
                  sub_rows, tiles_per_seq, n_prompt_tiles, sample_seq, sample_pos0):
    g_mix_ref, conv_w_ref, conv_b_ref, pool_scale_ref, g_ple_ref, g_final_ref = refs[:6]
    hbm_weight_refs = refs[6:11]
    (y_ref, ys_ref, conv_p_ref, pool_p_ref, conv_s_ref, pool_s_ref,
     u_hist_ref, v_hist_ref) = refs[11:19]
    bf16_weight_refs = refs[19:24]
    stage_ref, pool_stage_ref, sems = refs[24:]
    w_in_ref, pool_w_ref, w_out_ref, w_gate_ref, w_ple_ref = bf16_weight_refs
    weight_refs = (g_mix_ref, w_in_ref, conv_w_ref, conv_b_ref, pool_w_ref, pool_scale_ref,
                   w_out_ref, g_ple_ref, w_gate_ref, w_ple_ref, g_final_ref)
    t = pl.program_id(0)
    tile_in_seq = jax.lax.rem(t, tiles_per_seq)

    @pl.when(t == 0)
    def _():
        _cast_matmul_weights(hbm_weight_refs, bf16_weight_refs, stage_ref, pool_stage_ref, sems)

    @pl.when(jnp.logical_and(t < n_prompt_tiles, tile_in_seq == 0))
    def _():
        u_hist_ref[...] = jnp.zeros_like(u_hist_ref)
        v_hist_ref[...] = jnp.zeros_like(v_hist_ref)

    @pl.when(t < n_prompt_tiles)
    def _():
        rows = x_ref.shape[0]
        w = _Weights(weight_refs)
        tails = [(u_hist_ref[...], v_hist_ref[...])]
        stage_lists = []
        for k in range(rows // sub_rows):
            sl = pl.ds(k * sub_rows, sub_rows)

            def emit(out, sl=sl):
                y_ref[sl, :] = out

            stage_lists.append(_segment_stages(
                w, x_ref[sl, :], p_ref[sl, :], lambda k=k: [tails[k]],
                tile_in_seq * rows + k * sub_rows, lambda new: tails.append(new[0]), emit))
        _run_skewed(stage_lists, skew=2)
        u_tail, v_tail = tails[-1]
        u_hist_ref[...] = u_tail
        v_hist_ref[...] = v_tail
        seq = jax.lax.div(t, tiles_per_seq)
        conv_p_ref[seq] = u_tail[HIST_C_PAD - HIST_C:]
        for b in range(pool_p_ref.shape[1]):
            @pl.when(seq == b)
            def _(b=b):
                pool_p_ref[:, b, :] = v_tail[HIST_P_PAD - HIST_P:]

    @pl.when(t >= n_prompt_tiles)
    def _():
        streams = xs_ref.shape[0] // sample_seq

        def hists():
            zeros_c = jnp.zeros((HIST_C_PAD - HIST_C, W_A), jnp.float32)
            zeros_p = jnp.zeros((HIST_P_PAD - HIST_P, W_B), jnp.float32)
            return [(jnp.concatenate([zeros_c, cache_conv_ref[i]], axis=0),
                     jnp.concatenate([zeros_p, cache_pool_ref[:, i, :]], axis=0))
                    for i in range(streams)]

        def carry(tails):
            for i, (u_tail, v_tail) in enumerate(tails):
                conv_s_ref[i] = u_tail[HIST_C_PAD - HIST_C:]
                pool_s_ref[:, i, :] = v_tail[HIST_P_PAD - HIST_P:]

        def emit(out):
            ys_ref[...] = out

        stages = _segment_stages(_Weights(weight_refs), xs_ref[...], ps_ref[...], hists,
                                 sample_pos0, carry, emit)
        _run_skewed([stages], skew=1)


def _resident(shape):
    zeros = (0,) * len(shape)
    return pl.BlockSpec(shape, lambda *_: zeros, pipeline_mode=pl.Buffered(1))


def _small_param_specs():
    return [_resident((1, D_MODEL)), _resident((1, CONV_W * W_A)), _resident((1, W_A)),
            _resident((1, W_B)), _resident((1, D_MODEL)), _resident((1, D_MODEL))]


def _run_trunk(x, p, xs, ps, cache_conv, cache_pool, small_params, matmul_weights):
    batch, seq, _ = x.shape
    streams, sample_seq, _ = xs.shape
    tile, sub_rows = PROMPT_TILE, PROMPT_SUB_ROWS
    assert seq % tile == 0 and tile % sub_rows == 0 and sub_rows >= HIST_P_PAD
    assert sub_rows % sample_seq == 0 and sample_seq >= HIST_P_PAD and sample_seq % SUBLANES == 0
    per_tile = sub_rows // sample_seq
    assert streams % per_tile == 0
    n_prompt_tiles = batch * seq // tile
    n_sample_tiles = streams // per_tile
    prompt_tile = lambda t: (jnp.minimum(t, n_prompt_tiles - 1), 0)
    sample_tile = lambda t: (jnp.clip(t - n_prompt_tiles, 0, n_sample_tiles - 1), 0)
    sample_streams = lambda t: sample_tile(t) + (0,)
    sample_rows_streams = lambda t: (0,) + sample_tile(t)
    whole = lambda t: (0, 0, 0)
    f32, bf16 = jnp.float32, jnp.bfloat16
    y, ys, conv_p, pool_p, conv_s, pool_s = pl.pallas_call(
        functools.partial(_trunk_kernel, sub_rows=sub_rows, tiles_per_seq=seq // tile,
                          n_prompt_tiles=n_prompt_tiles, sample_seq=sample_seq,
                          sample_pos0=PAST_LEN),
        grid=(n_prompt_tiles + n_sample_tiles,),
        in_specs=[pl.BlockSpec((tile, D_MODEL), prompt_tile),
                  pl.BlockSpec((tile, PLE_DIM), prompt_tile),
                  pl.BlockSpec((sub_rows, D_MODEL), sample_tile),
                  pl.BlockSpec((sub_rows, PLE_DIM), sample_tile),
                  pl.BlockSpec((per_tile, HIST_C, W_A), sample_streams),
                  pl.BlockSpec((HIST_P, per_tile, W_B), sample_rows_streams)]
        + _small_param_specs() + [pl.BlockSpec(memory_space=pl.ANY)] * len(matmul_weights),
        out_specs=[pl.BlockSpec((tile, D_MODEL), prompt_tile),
                   pl.BlockSpec((sub_rows, D_MODEL), sample_tile),
                   pl.BlockSpec((batch, HIST_C, W_A), whole),
                   pl.BlockSpec((HIST_P, batch, W_B), whole),
                   pl.BlockSpec((per_tile, HIST_C, W_A), sample_streams),
                   pl.BlockSpec((HIST_P, per_tile, W_B), sample_rows_streams)],
        out_shape=[
            jax.ShapeDtypeStruct((batch * seq, D_MODEL), f32),
            jax.ShapeDtypeStruct((streams * sample_seq, D_MODEL), f32),
            jax.ShapeDtypeStruct((batch, HIST_C, W_A), f32),
            jax.ShapeDtypeStruct((HIST_P, batch, W_B), f32),
            jax.ShapeDtypeStruct((streams, HIST_C, W_A), f32),
            jax.ShapeDtypeStruct((HIST_P, streams, W_B), f32),
        ],
        scratch_shapes=[
            pltpu.VMEM((HIST_C_PAD, W_A), f32), pltpu.VMEM((HIST_P_PAD, W_B), f32),
            pltpu.VMEM((D_MODEL, PROJ_W), bf16),
            pltpu.VMEM((2, 2 * GC, 2 * GC), bf16),
            pltpu.VMEM((D_MODEL, D_MODEL), bf16),
            pltpu.VMEM((D_MODEL, D_MODEL), bf16),
            pltpu.VMEM((PLE_DIM, D_MODEL), bf16),
            pltpu.VMEM((CAST_BUFFERS, CAST_CHUNK_ROWS, PROJ_W), f32),
            pltpu.VMEM((len(POOL_WINDOWS), GC, GC), f32),
            pltpu.SemaphoreType.DMA((CAST_BUFFERS + 1,)),
        ],
        compiler_params=pltpu.CompilerParams(
            dimension_semantics=("arbitrary",),
            vmem_limit_bytes=VMEM_LIMIT_BYTES),
        name="trunk",
    )(x.reshape(batch * seq, D_MODEL), p.reshape(batch * seq, PLE_DIM),
      xs.reshape(streams * sample_seq, D_MODEL), ps.reshape(streams * sample_seq, PLE_DIM),
      cache_conv, cache_pool.transpose(1, 0, 2), *small_params, *matmul_weights)
    return (y.reshape(batch, seq, D_MODEL), ys.reshape(streams, sample_seq, D_MODEL),
            conv_p, pool_p.transpose(1, 0, 2), conv_s, pool_s.transpose(1, 0, 2))


def kernel(x_prompt, x_sample, cache_conv, cache_pool, p_prompt, p_sample, g_mix, w_in, conv_w,
           conv_b, pool_w, pool_scale, w_out, g_ple, w_ple_gate, w_ple, g_final):
    depth = g_mix.shape[0]
    assert depth == 1, "single-layer trunk"
    small_params = (g_mix, conv_w.reshape(1, CONV_W * W_A), conv_b, pool_scale, g_ple,
                    g_final.reshape(1, D_MODEL))
    matmul_weights = (w_in[0], pool_w[0], w_out[0], w_ple_gate[0], w_ple[0])
    y_prompt, y_sample, conv_p, pool_p, conv_s, pool_s = _run_trunk(
        x_prompt, p_prompt[0], x_sample, p_sample[0], cache_conv[0], cache_pool[0],
        small_params, matmul_weights)
    return y_prompt, y_sample, conv_p[None], pool_p[None], conv_s[None], pool_s[None]
```

```python
import functools

import jax
import jax.numpy as jnp
from jax.experimental import pallas as pl
from jax.experimental.pallas import tpu as pltpu

D_MODEL = 1024
W_A = 512
W_B = 512
CONV_W = 3
HIST_C = CONV_W - 1
POOL_WINDOWS = (2, 4, 8, 16)
GC = W_B // len(POOL_WINDOWS)
HIST_P = max(POOL_WINDOWS) - 1
PLE_DIM = 256
PROJ_W = 4 * W_A + 2 * W_B
EPS = 1e-6
PAST_LEN = 2048
LOG2_E = 1.4426950408889634

SUBLANES = 8
HIST_C_PAD = SUBLANES
HIST_P_PAD = 2 * SUBLANES
PROMPT_TILE = 1024
PROMPT_SUB_ROWS = 256
CAST_CHUNK_ROWS = 128
CAST_BUFFERS = 4
VMEM_LIMIT_BYTES = 56 * 1024 * 1024


def _rmsnorm(x, g):
    ms = jnp.mean(x * x, axis=-1, keepdims=True)
    return x * jax.lax.rsqrt(ms + EPS) * g


def _sigmoid(z):
    return 1.0 / (1.0 + jnp.exp2(z * -LOG2_E))


def _silu(z):
    return z / (1.0 + jnp.exp2(z * -LOG2_E))


def _dot(a, w):
    return jnp.dot(a.astype(jnp.bfloat16), w, preferred_element_type=jnp.float32)


def _shift_rows(ext, k, pad):
    return pltpu.roll(ext, k, axis=0)[pad:]


def _conv_mixer(hA, bA, cA, zA, u_hist, conv_w, conv_b):
    u = cA * hA
    ext = jnp.concatenate([u_hist, u], axis=0)
    taps = [conv_w[:, k * W_A:(k + 1) * W_A] for k in range(CONV_W)]
    c = (_shift_rows(ext, 2, HIST_C_PAD) * taps[0] + _shift_rows(ext, 1, HIST_C_PAD) * taps[1]
         + u * taps[2] + conv_b)
    return bA * c * _silu(zA), u


def _pool_group(ext, v, window, pos0):
    rows = v.shape[0]
    total = ext.shape[0]
    s = ext
    span = 1
    while span < window:
        s = s + jnp.concatenate([s[total - span:], s[:total - span]], axis=0)
        span *= 2
    s = s[HIST_P_PAD:]
    head = HIST_P_PAD
    r = jax.lax.broadcasted_iota(jnp.int32, (head, GC), 0)
    cnt = jnp.minimum(pos0 + r + 1, window).astype(jnp.float32)
    mean_head = s[:head] / cnt
    if rows > head:
        mean = jnp.concatenate([mean_head, s[head:] * (1.0 / window)], axis=0)
    else:
        mean = mean_head
    return mean - v


def _pooled(vB, v_hist, pos0):
    pooled = []
    for g, window in enumerate(POOL_WINDOWS):
        sl = slice(g * GC, (g + 1) * GC)
        ext = jnp.concatenate([v_hist[:, sl], vB[:, sl]], axis=0)
        pooled.append(_pool_group(ext, vB[:, sl], window, pos0))
    return jnp.concatenate(pooled, axis=-1)


def _split_proj(proj):
    hA = proj[:, 0 * W_A:1 * W_A]
    bA = proj[:, 1 * W_A:2 * W_A]
    cA = proj[:, 2 * W_A:3 * W_A]
    zA = proj[:, 3 * W_A:4 * W_A]
    vB = proj[:, 4 * W_A:4 * W_A + W_B]
    zB = proj[:, 4 * W_A + W_B:]
    return hA, bA, cA, zA, vB, zB


class _Weights:
    def __init__(self, refs):
        (self.g_mix, self.w_in, self.conv_w, self.conv_b, self.pool_w, self.pool_scale,
         self.w_out, self.g_ple, self.w_gate, self.w_ple, self.g_final) = [r[...] for r in refs]


def _segment_stages(w, x, p, hists, pos0, carry, emit):
    s = {}

    def norm_in():
        s["hn"] = _rmsnorm(x, w.g_mix).astype(jnp.bfloat16)

    def project():
        s["proj"] = _dot(s.pop("hn"), w.w_in)

    def mix():
        hA, bA, cA, zA, vB, zB = _split_proj(s.pop("proj"))
        stream_hists = hists()
        seq = x.shape[0] // len(stream_hists)
        yA, pooled, tails = [], [], []
        for i, (u_hist, v_hist) in enumerate(stream_hists):
            sl = slice(i * seq, (i + 1) * seq)
            yA_i, u = _conv_mixer(hA[sl], bA[sl], cA[sl], zA[sl], u_hist, w.conv_w, w.conv_b)
            yA.append(yA_i)
            pooled.append(_pooled(vB[sl], v_hist, pos0))
            tails.append((u[seq - HIST_C_PAD:], vB[sl][seq - HIST_P_PAD:]))
        carry(tails)
        s["yA"] = jnp.concatenate(yA, axis=0).astype(jnp.bfloat16)
        s["pooled"] = jnp.concatenate(pooled, axis=0).astype(jnp.bfloat16)
        s["gB"] = _silu(zB)

    def project_pool():
        pooled = s.pop("pooled")
        s["mixed"] = [_dot(pooled[:, 2 * GC * i:2 * GC * (i + 1)], w.pool_w[i])
                      for i in range(len(POOL_WINDOWS) // 2)]

    def gate_pool():
        yB = jnp.concatenate(s.pop("mixed"), axis=-1) * w.pool_scale * s.pop("gB")
        s["y_mix"] = jnp.concatenate([s.pop("yA"), yB.astype(jnp.bfloat16)], axis=-1)

    def project_out():
        s["o"] = _dot(s.pop("y_mix"), w.w_out)

    def norm_gate():
        s["h"] = x + s.pop("o")
        s["hn2"] = _rmsnorm(s["h"], w.g_ple).astype(jnp.bfloat16)

    def project_gate():
        s["z"] = _dot(s.pop("hn2"), w.w_gate)
        s["e"] = _dot(p, w.w_ple)

    def finish():
        h = s.pop("h") + s.pop("e") * _sigmoid(s.pop("z"))
        emit(_rmsnorm(h, w.g_final))

    return [norm_in, project, mix, project_pool, gate_pool, project_out, norm_gate,
            project_gate, finish]


def _run_skewed(stage_lists, skew):
    depth = len(stage_lists[0])
    for slot in range(skew * (len(stage_lists) - 1) + depth):
        for k, stages in enumerate(stage_lists):
            j = slot - skew * k
            if 0 <= j < depth:
                stages[j]()


def _cast_matmul_weights(hbm_refs, bf16_refs, stage_ref, pool_stage_ref, sems):
    w_in, pool_w, w_out, w_gate, w_ple = hbm_refs
    w_in_s, pool_w_s, w_out_s, w_gate_s, w_ple_s = bf16_refs
    n_buf = stage_ref.shape[0]
    pool_copy = pltpu.make_async_copy(pool_w, pool_stage_ref, sems.at[n_buf])
    pool_copy.start()
    chunks = []
    for src, dst in ((w_in, w_in_s), (w_out, w_out_s), (w_gate, w_gate_s), (w_ple, w_ple_s)):
        n_rows, width = src.shape
        for r in range(0, n_rows, CAST_CHUNK_ROWS):
            rows = pl.ds(r, CAST_CHUNK_ROWS)
            chunks.append((src.at[rows], dst.at[rows], width))

    def copy(i):
        src, _, width = chunks[i]
        slot = i % n_buf
        return pltpu.make_async_copy(src, stage_ref.at[slot, :, pl.ds(0, width)], sems.at[slot])

    for i in range(min(n_buf - 1, len(chunks))):
        copy(i).start()
    for i, (_, dst, width) in enumerate(chunks):
        if i + n_buf - 1 < len(chunks):
            copy(i + n_buf - 1).start()
        copy(i).wait()
        dst[...] = stage_ref[i % n_buf, :, :width].astype(jnp.bfloat16)
    pool_copy.wait()
    pool_w_s[...] = jnp.zeros_like(pool_w_s)
    for g in range(len(POOL_WINDOWS)):
        pair, half = divmod(g, 2)
        block = slice(half * GC, (half + 1) * GC)
        pool_w_s[pair, block, block] = pool_stage_ref[g].astype(jnp.bfloat16)


def _trunk_kernel(x_ref, p_ref, xs_ref, ps_ref, cache_conv_ref, cache_pool_ref, *refs,
                  sub_rows, tiles_per_seq, n_prompt_tiles, sample_seq, sample_pos0):
    g_mix_ref, conv_w_ref, conv_b_ref, pool_scale_ref, g_ple_ref, g_final_ref = refs[:6]
    hbm_weight_refs = refs[6:11]
    (y_ref, ys_ref, conv_p_ref, pool_p_ref, conv_s_ref, pool_s_ref,
     u_hist_ref, v_hist_ref) = refs[11:19]
    bf16_weight_refs = refs[19:24]
    stage_ref, pool_stage_ref, sems = refs[24:]
    w_in_ref, pool_w_ref, w_out_ref, w_gate_ref, w_ple_ref = bf16_weight_refs
    weight_refs = (g_mix_ref, w_in_ref, conv_w_ref, conv_b_ref, pool_w_ref, pool_scale_ref,
                   w_out_ref, g_ple_ref, w_gate_ref, w_ple_ref, g_final_ref)
    t = pl.program_id(0)
    tile_in_seq = jax.lax.rem(t, tiles_per_seq)

    @pl.when(t == 0)
    def _():
        _cast_matmul_weights(hbm_weight_refs, bf16_weight_refs, stage_ref, pool_stage_ref, sems)

    @pl.when(jnp.logical_and(t < n_prompt_tiles, tile_in_seq == 0))
    def _():
        u_hist_ref[...] = jnp.zeros_like(u_hist_ref)
        v_hist_ref[...] = jnp.zeros_like(v_hist_ref)

    @pl.when(t < n_prompt_tiles)
    def _():
        rows = x_ref.shape[0]
        w = _Weights(weight_refs)
        tails = [(u_hist_ref[...], v_hist_ref[...])]
        stage_lists = []
        for k in range(rows // sub_rows):
            sl = pl.ds(k * sub_rows, sub_rows)

            def emit(out, sl=sl):
                y_ref[sl, :] = out

            stage_lists.append(_segment_stages(
                w, x_ref[sl, :], p_ref[sl, :], lambda k=k: [tails[k]],
                tile_in_seq * rows + k * sub_rows, lambda new: tails.append(new[0]), emit))
        _run_skewed(stage_lists, skew=2)
        u_tail, v_tail = tails[-1]
        u_hist_ref[...] = u_tail
        v_hist_ref[...] = v_tail
        seq = jax.lax.div(t, tiles_per_seq)
        conv_p_ref[seq] = u_tail[HIST_C_PAD - HIST_C:]
        for b in range(pool_p_ref.shape[1]):
            @pl.when(seq == b)
            def _(b=b):
                pool_p_ref[:, b, :] = v_tail[HIST_P_PAD - HIST_P:]

    @pl.when(t >= n_prompt_tiles)
    def _():
        streams = xs_ref.shape[0] // sample_seq

        def hists():
            zeros_c = jnp.zeros((HIST_C_PAD - HIST_C, W_A), jnp.float32)
            zeros_p = jnp.zeros((HIST_P_PAD - HIST_P, W_B), jnp.float32)
            return [(jnp.concatenate([zeros_c, cache_conv_ref[i]], axis=0),
                     jnp.concatenate([zeros_p, cache_pool_ref[:, i, :]], axis=0))
                    for i in range(streams)]

        def carry(tails):
            for i, (u_tail, v_tail) in enumerate(tails):
                conv_s_ref[i] = u_tail[HIST_C_PAD - HIST_C:]
                pool_s_ref[:, i, :] = v_tail[HIST_P_PAD - HIST_P:]

        def emit(out):
            ys_ref[...] = out

        stages = _segment_stages(_Weights(weight_refs), xs_ref[...], ps_ref[...], hists,
                                 sample_pos0, carry, emit)
        _run_skewed([stages], skew=1)


def _resident(shape):
    zeros = (0,) * len(shape)
    return pl.BlockSpec(shape, lambda *_: zeros, pipeline_mode=pl.Buffered(1))


def _small_param_specs():
    return [_resident((1, D_MODEL)), _resident((1, CONV_W * W_A)), _resident((1, W_A)),
            _resident((1, W_B)), _resident((1, D_MODEL)), _resident((1, D_MODEL))]


def _run_trunk(x, p, xs, ps, cache_conv, cache_pool, small_params, matmul_weights):
    batch, seq, _ = x.shape
    streams, sample_seq, _ = xs.shape
    tile, sub_rows = PROMPT_TILE, PROMPT_SUB_ROWS
    assert seq % tile == 0 and tile % sub_rows == 0 and sub_rows >= HIST_P_PAD
    assert sub_rows % sample_seq == 0 and sample_seq >= HIST_P_PAD and sample_seq % SUBLANES == 0
    per_tile = sub_rows // sample_seq
    assert streams % per_tile == 0
    n_prompt_tiles = batch * seq // tile
    n_sample_tiles = streams // per_tile
    prompt_tile = lambda t: (jnp.minimum(t, n_prompt_tiles - 1), 0)
    sample_tile = lambda t: (jnp.clip(t - n_prompt_tiles, 0, n_sample_tiles - 1), 0)
    sample_streams = lambda t: sample_tile(t) + (0,)
    sample_rows_streams = lambda t: (0,) + sample_tile(t)
    whole = lambda t: (0, 0, 0)
    f32, bf16 = jnp.float32, jnp.bfloat16
    y, ys, conv_p, pool_p, conv_s, pool_s = pl.pallas_call(
        functools.partial(_trunk_kernel, sub_rows=sub_rows, tiles_per_seq=seq // tile,
                          n_prompt_tiles=n_prompt_tiles, sample_seq=sample_seq,
                          sample_pos0=PAST_LEN),
        grid=(n_prompt_tiles + n_sample_tiles,),
        in_specs=[pl.BlockSpec((tile, D_MODEL), prompt_tile),
                  pl.BlockSpec((tile, PLE_DIM), prompt_tile),
                  pl.BlockSpec((sub_rows, D_MODEL), sample_tile),
                  pl.BlockSpec((sub_rows, PLE_DIM), sample_tile),
                  pl.BlockSpec((per_tile, HIST_C, W_A), sample_streams),
                  pl.BlockSpec((HIST_P, per_tile, W_B), sample_rows_streams)]
        + _small_param_specs() + [pl.BlockSpec(memory_space=pl.ANY)] * len(matmul_weights),
        out_specs=[pl.BlockSpec((tile, D_MODEL), prompt_tile),
                   pl.BlockSpec((sub_rows, D_MODEL), sample_tile),
                   pl.BlockSpec((batch, HIST_C, W_A), whole),
                   pl.BlockSpec((HIST_P, batch, W_B), whole),
                   pl.BlockSpec((per_tile, HIST_C, W_A), sample_streams),
                   pl.BlockSpec((HIST_P, per_tile, W_B), sample_rows_streams)],
        out_shape=[
            jax.ShapeDtypeStruct((batch * seq, D_MODEL), f32),
            jax.ShapeDtypeStruct((streams * sample_seq, D_MODEL), f32),
            jax.ShapeDtypeStruct((batch, HIST_C, W_A), f32),
            jax.ShapeDtypeStruct((HIST_P, batch, W_B), f32),
            jax.ShapeDtypeStruct((streams, HIST_C, W_A), f32),
            jax.ShapeDtypeStruct((HIST_P, streams, W_B), f32),
        ],
        scratch_shapes=[
            pltpu.VMEM((HIST_C_PAD, W_A), f32), pltpu.VMEM((HIST_P_PAD, W_B), f32),
            pltpu.VMEM((D_MODEL, PROJ_W), bf16),
            pltpu.VMEM((2, 2 * GC, 2 * GC), bf16),
            pltpu.VMEM((D_MODEL, D_MODEL), bf16),
            pltpu.VMEM((D_MODEL, D_MODEL), bf16),
            pltpu.VMEM((PLE_DIM, D_MODEL), bf16),
            pltpu.VMEM((CAST_BUFFERS, CAST_CHUNK_ROWS, PROJ_W), f32),
            pltpu.VMEM((len(POOL_WINDOWS), GC, GC), f32),
            pltpu.SemaphoreType.DMA((CAST_BUFFERS + 1,)),
        ],
        compiler_params=pltpu.CompilerParams(
            dimension_semantics=("arbitrary",),
            vmem_limit_bytes=VMEM_LIMIT_BYTES),
        name="trunk",
    )(x.reshape(batch * seq, D_MODEL), p.reshape(batch * seq, PLE_DIM),
      xs.reshape(streams * sample_seq, D_MODEL), ps.reshape(streams * sample_seq, PLE_DIM),
      cache_conv, cache_pool.transpose(1, 0, 2), *small_params, *matmul_weights)
    return (y.reshape(batch, seq, D_MODEL), ys.reshape(streams, sample_seq, D_MODEL),
            conv_p, pool_p.transpose(1, 0, 2), conv_s, pool_s.transpose(1, 0, 2))


def kernel(x_prompt, x_sample, cache_conv, cache_pool, p_prompt, p_sample, g_mix, w_in, conv_w,
           conv_b, pool_w, pool_scale, w_out, g_ple, w_ple_gate, w_ple, g_final):
    depth = g_mix.shape[0]
    assert depth == 1, "single-layer trunk"
    small_params = (g_mix, conv_w.reshape(1, CONV_W * W_A), conv_b, pool_scale, g_ple,
                    g_final.reshape(1, D_MODEL))
    matmul_weights = (w_in[0], pool_w[0], w_out[0], w_ple_gate[0], w_ple[0])
    y_prompt, y_sample, conv_p, pool_p, conv_s, pool_s = _run_trunk(
        x_prompt, p_prompt[0], x_sample, p_sample[0], cache_conv[0], cache_pool[0],
        small_params, matmul_weights)
    return y_prompt, y_sample, conv_p[None], pool_p[None], conv_s[None], pool_s[None]
```

```python
import functools

import jax
import jax.numpy as jnp
from jax.experimental import pallas as pl
from jax.experimental.pallas import tpu as pltpu

D_MODEL = 1024
W_A = 512
W_B = 512
CONV_W = 3
HIST_C = CONV_W - 1
POOL_WINDOWS = (2, 4, 8, 16)
GC = W_B // len(POOL_WINDOWS)
HIST_P = max(POOL_WINDOWS) - 1
PLE_DIM = 256
PROJ_W = 4 * W_A + 2 * W_B
EPS = 1e-6
PAST_LEN = 2048
LOG2_E = 1.4426950408889634

SUBLANES = 8
HIST_C_PAD = SUBLANES
HIST_P_PAD = 2 * SUBLANES
PROMPT_TILE = 1024
PROMPT_SUB_ROWS = 256
CAST_CHUNK_ROWS = 128
VMEM_LIMIT_BYTES = 56 * 1024 * 1024


def _rmsnorm(x, g):
    ms = jnp.mean(x * x, axis=-1, keepdims=True)
    return x * jax.lax.rsqrt(ms + EPS) * g


def _sigmoid(z):
    return 1.0 / (1.0 + jnp.exp2(z * -LOG2_E))


def _silu(z):
    return z / (1.0 + jnp.exp2(z * -LOG2_E))


def _dot(a, w):
    return jnp.dot(a.astype(jnp.bfloat16), w, preferred_element_type=jnp.float32)


def _shift_rows(ext, k, pad):
    return pltpu.roll(ext, k, axis=0)[pad:]


def _conv_mixer(hA, bA, cA, zA, u_hist, conv_w, conv_b):
    u = cA * hA
    ext = jnp.concatenate([u_hist, u], axis=0)
    taps = [conv_w[:, k * W_A:(k + 1) * W_A] for k in range(CONV_W)]
    c = (_shift_rows(ext, 2, HIST_C_PAD) * taps[0] + _shift_rows(ext, 1, HIST_C_PAD) * taps[1]
         + u * taps[2] + conv_b)
    return bA * c * _silu(zA), u


def _pool_group(ext, v, window, pos0):
    rows = v.shape[0]
    total = ext.shape[0]
    s = ext
    span = 1
    while span < window:
        s = s + jnp.concatenate([s[total - span:], s[:total - span]], axis=0)
        span *= 2
    s = s[HIST_P_PAD:]
    head = HIST_P_PAD
    r = jax.lax.broadcasted_iota(jnp.int32, (head, GC), 0)
    cnt = jnp.minimum(pos0 + r + 1, window).astype(jnp.float32)
    mean_head = s[:head] / cnt
    if rows > head:
        mean = jnp.concatenate([mean_head, s[head:] * (1.0 / window)], axis=0)
    else:
        mean = mean_head
    return mean - v


def _pooled(vB, v_hist, pos0):
    pooled = []
    for g, window in enumerate(POOL_WINDOWS):
        sl = slice(g * GC, (g + 1) * GC)
        ext = jnp.concatenate([v_hist[:, sl], vB[:, sl]], axis=0)
        pooled.append(_pool_group(ext, vB[:, sl], window, pos0))
    return jnp.concatenate(pooled, axis=-1)


def _split_proj(proj):
    hA = proj[:, 0 * W_A:1 * W_A]
    bA = proj[:, 1 * W_A:2 * W_A]
    cA = proj[:, 2 * W_A:3 * W_A]
    zA = proj[:, 3 * W_A:4 * W_A]
    vB = proj[:, 4 * W_A:4 * W_A + W_B]
    zB = proj[:, 4 * W_A + W_B:]
    return hA, bA, cA, zA, vB, zB


class _Weights:
    def __init__(self, refs):
        (self.g_mix, self.w_in, self.conv_w, self.conv_b, self.pool_w, self.pool_scale,
         self.w_out, self.g_ple, self.w_gate, self.w_ple, self.g_final) = [r[...] for r in refs]


def _segment_stages(w, x, p, hists, pos0, carry, emit):
    s = {}

    def norm_in():
        s["hn"] = _rmsnorm(x, w.g_mix).astype(jnp.bfloat16)

    def project():
        s["proj"] = _dot(s.pop("hn"), w.w_in)

    def mix():
        hA, bA, cA, zA, vB, zB = _split_proj(s.pop("proj"))
        stream_hists = hists()
        seq = x.shape[0] // len(stream_hists)
        yA, pooled, tails = [], [], []
        for i, (u_hist, v_hist) in enumerate(stream_hists):
            sl = slice(i * seq, (i + 1) * seq)
            yA_i, u = _conv_mixer(hA[sl], bA[sl], cA[sl], zA[sl], u_hist, w.conv_w, w.conv_b)
            yA.append(yA_i)
            pooled.append(_pooled(vB[sl], v_hist, pos0))
            tails.append((u[seq - HIST_C_PAD:], vB[sl][seq - HIST_P_PAD:]))
        carry(tails)
        s["yA"] = jnp.concatenate(yA, axis=0).astype(jnp.bfloat16)
        s["pooled"] = jnp.concatenate(pooled, axis=0).astype(jnp.bfloat16)
        s["gB"] = _silu(zB)

    def project_pool():
        pooled = s.pop("pooled")
        s["mixed"] = [_dot(pooled[:, 2 * GC * i:2 * GC * (i + 1)], w.pool_w[i])
                      for i in range(len(POOL_WINDOWS) // 2)]

    def gate_pool():
        yB = jnp.concatenate(s.pop("mixed"), axis=-1) * w.pool_scale * s.pop("gB")
        s["y_mix"] = jnp.concatenate([s.pop("yA"), yB.astype(jnp.bfloat16)], axis=-1)

    def project_out():
        s["o"] = _dot(s.pop("y_mix"), w.w_out)

    def norm_gate():
        s["h"] = x + s.pop("o")
        s["hn2"] = _rmsnorm(s["h"], w.g_ple).astype(jnp.bfloat16)

    def project_ple():
        s["e"] = _dot(p, w.w_ple)

    def project_gate():
        s["z"] = _dot(s.pop("hn2"), w.w_gate)

    def finish():
        h = s.pop("h") + s.pop("e") * _sigmoid(s.pop("z"))
        emit(_rmsnorm(h, w.g_final))

    return project_ple, [norm_in, project, mix, project_pool, gate_pool, project_out, norm_gate,
                         project_gate, finish]


def _run_skewed(stage_lists, skew):
    depth = len(stage_lists[0])
    for slot in range(skew * (len(stage_lists) - 1) + depth):
        for k, stages in enumerate(stage_lists):
            j = slot - skew * k
            if 0 <= j < depth:
                stages[j]()


def _cast_matmul_weights(hbm_refs, bf16_refs, stage_ref, pool_stage_ref, sems):
    w_in, pool_w, w_out, w_gate, w_ple = hbm_refs
    w_in_s, pool_w_s, w_out_s, w_gate_s, w_ple_s = bf16_refs
    stage_rows, width = stage_ref.shape
    n_buf = stage_rows // CAST_CHUNK_ROWS
    pool_copy = pltpu.make_async_copy(pool_w, pool_stage_ref, sems.at[n_buf])
    pool_copy.start()
    chunks = []
    for src, dst in ((w_in, w_in_s), (w_out, w_out_s), (w_gate, w_gate_s), (w_ple, w_ple_s)):
        for r in range(0, src.shape[0], CAST_CHUNK_ROWS):
            for c in range(0, src.shape[1], width):
                block = (pl.ds(r, CAST_CHUNK_ROWS), pl.ds(c, width))
                chunks.append((src.at[block], dst.at[block]))

    def slot_ref(i):
        return stage_ref.at[pl.ds((i % n_buf) * CAST_CHUNK_ROWS, CAST_CHUNK_ROWS)]

    def copy(i):
        return pltpu.make_async_copy(chunks[i][0], slot_ref(i), sems.at[i % n_buf])

    for i in range(min(n_buf - 1, len(chunks))):
        copy(i).start()
    for i, (_, dst) in enumerate(chunks):
        if i + n_buf - 1 < len(chunks):
            copy(i + n_buf - 1).start()
        copy(i).wait()
        dst[...] = slot_ref(i)[...].astype(jnp.bfloat16)
    pool_copy.wait()
    pool_w_s[...] = jnp.zeros_like(pool_w_s)
    for g in range(len(POOL_WINDOWS)):
        pair, half = divmod(g, 2)
        block = slice(half * GC, (half + 1) * GC)
        pool_w_s[pair, block, block] = pool_stage_ref[g].astype(jnp.bfloat16)


def _trunk_kernel(x_ref, p_ref, xs_ref, ps_ref, cache_conv_ref, cache_pool_ref, *refs,
                  sub_rows, tiles_per_seq, n_prompt_tiles, sample_seq, sample_pos0):
    g_mix_ref, conv_w_ref, conv_b_ref, pool_scale_ref, g_ple_ref, g_final_ref = refs[:6]
    hbm_weight_refs = refs[6:11]
    (y_ref, ys_ref, conv_p_ref, pool_p_ref, conv_s_ref, pool_s_ref,
     u_hist_ref, v_hist_ref) = refs[11:19]
    bf16_weight_refs = refs[19:24]
    pool_stage_ref, sems = refs[24:]
    w_in_ref, pool_w_ref, w_out_ref, w_gate_ref, w_ple_ref = bf16_weight_refs
    weight_refs = (g_mix_ref, w_in_ref, conv_w_ref, conv_b_ref, pool_w_ref, pool_scale_ref,
                   w_out_ref, g_ple_ref, w_gate_ref, w_ple_ref, g_final_ref)
    t = pl.program_id(0)
    tile_in_seq = jax.lax.rem(t, tiles_per_seq)

    @pl.when(t == 0)
    def _():
        _cast_matmul_weights(hbm_weight_refs, bf16_weight_refs, y_ref, pool_stage_ref, sems)

    @pl.when(jnp.logical_and(t < n_prompt_tiles, tile_in_seq == 0))
    def _():
        u_hist_ref[...] = jnp.zeros_like(u_hist_ref)
        v_hist_ref[...] = jnp.zeros_like(v_hist_ref)

    @pl.when(t < n_prompt_tiles)
    def _():
        rows = x_ref.shape[0]
        w = _Weights(weight_refs)
        tails = [(u_hist_ref[...], v_hist_ref[...])]
        stage_lists = []
        for k in range(rows // sub_rows):
            sl = pl.ds(k * sub_rows, sub_rows)

            def emit(out, sl=sl):
                y_ref[sl, :] = out

            stage_lists.append(_segment_stages(
                w, x_ref[sl, :], p_ref[sl, :], lambda k=k: [tails[k]],
                tile_in_seq * rows + k * sub_rows, lambda new: tails.append(new[0]), emit))
        for early, _ in stage_lists:
            early()
        _run_skewed([stages for _, stages in stage_lists], skew=2)
        u_tail, v_tail = tails[-1]
        u_hist_ref[...] = u_tail
        v_hist_ref[...] = v_tail
        seq = jax.lax.div(t, tiles_per_seq)
        conv_p_ref[seq] = u_tail[HIST_C_PAD - HIST_C:]
        for b in range(pool_p_ref.shape[1]):
            @pl.when(seq == b)
            def _(b=b):
                pool_p_ref[:, b, :] = v_tail[HIST_P_PAD - HIST_P:]

    @pl.when(t >= n_prompt_tiles)
    def _():
        streams = xs_ref.shape[0] // sample_seq

        def hists():
            zeros_c = jnp.zeros((HIST_C_PAD - HIST_C, W_A), jnp.float32)
            zeros_p = jnp.zeros((HIST_P_PAD - HIST_P, W_B), jnp.float32)
            return [(jnp.concatenate([zeros_c, cache_conv_ref[i]], axis=0),
                     jnp.concatenate([zeros_p, cache_pool_ref[:, i, :]], axis=0))
                    for i in range(streams)]

        def carry(tails):
            for i, (u_tail, v_tail) in enumerate(tails):
                conv_s_ref[i] = u_tail[HIST_C_PAD - HIST_C:]
                pool_s_ref[:, i, :] = v_tail[HIST_P_PAD - HIST_P:]

        def emit(out):
            ys_ref[...] = out

        early, stages = _segment_stages(_Weights(weight_refs), xs_ref[...], ps_ref[...], hists,
                                        sample_pos0, carry, emit)
        early()
        _run_skewed([stages], skew=1)


def _resident(shape):
    zeros = (0,) * len(shape)
    return pl.BlockSpec(shape, lambda *_: zeros, pipeline_mode=pl.Buffered(1))


def _small_param_specs():
    return [_resident((1, D_MODEL)), _resident((1, CONV_W * W_A)), _resident((1, W_A)),
            _resident((1, W_B)), _resident((1, D_MODEL)), _resident((1, D_MODEL))]


def _run_trunk(x, p, xs, ps, cache_conv, cache_pool, small_params, matmul_weights):
    batch, seq, _ = x.shape
    streams, sample_seq, _ = xs.shape
    tile, sub_rows = PROMPT_TILE, PROMPT_SUB_ROWS
    assert seq % tile == 0 and tile % sub_rows == 0 and sub_rows >= HIST_P_PAD
    assert sub_rows % sample_seq == 0 and sample_seq >= HIST_P_PAD and sample_seq % SUBLANES == 0
    assert tile % CAST_CHUNK_ROWS == 0 and all(
        w.shape[0] % CAST_CHUNK_ROWS == 0 and w.shape[1] % D_MODEL == 0
        for w in matmul_weights if w.ndim == 2), "weights are staged in output-block row slots"
    per_tile = sub_rows // sample_seq
    assert streams % per_tile == 0
    n_prompt_tiles = batch * seq // tile
    n_sample_tiles = streams // per_tile
    prompt_tile = lambda t: (jnp.minimum(t, n_prompt_tiles - 1), 0)
    sample_tile = lambda t: (jnp.clip(t - n_prompt_tiles, 0, n_sample_tiles - 1), 0)
    sample_streams = lambda t: sample_tile(t) + (0,)
    sample_rows_streams = lambda t: (0,) + sample_tile(t)
    whole = lambda t: (0, 0, 0)
    f32, bf16 = jnp.float32, jnp.bfloat16
    y, ys, conv_p, pool_p, conv_s, pool_s = pl.pallas_call(
        functools.partial(_trunk_kernel, sub_rows=sub_rows, tiles_per_seq=seq // tile,
                          n_prompt_tiles=n_prompt_tiles, sample_seq=sample_seq,
                          sample_pos0=PAST_LEN),
        grid=(n_prompt_tiles + n_sample_tiles,),
        in_specs=[pl.BlockSpec((tile, D_MODEL), prompt_tile),
                  pl.BlockSpec((tile, PLE_DIM), prompt_tile),
                  pl.BlockSpec((sub_rows, D_MODEL), sample_tile),
                  pl.BlockSpec((sub_rows, PLE_DIM), sample_tile),
                  pl.BlockSpec((per_tile, HIST_C, W_A), sample_streams),
                  pl.BlockSpec((HIST_P, per_tile, W_B), sample_rows_streams)]
        + _small_param_specs() + [pl.BlockSpec(memory_space=pl.ANY)] * len(matmul_weights),
        out_specs=[pl.BlockSpec((tile, D_MODEL), prompt_tile),
                   pl.BlockSpec((sub_rows, D_MODEL), sample_tile),
                   pl.BlockSpec((batch, HIST_C, W_A), whole),
                   pl.BlockSpec((HIST_P, batch, W_B), whole),
                   pl.BlockSpec((per_tile, HIST_C, W_A), sample_streams),
                   pl.BlockSpec((HIST_P, per_tile, W_B), sample_rows_streams)],
        out_shape=[
            jax.ShapeDtypeStruct((batch * seq, D_MODEL), f32),
            jax.ShapeDtypeStruct((streams * sample_seq, D_MODEL), f32),
            jax.ShapeDtypeStruct((batch, HIST_C, W_A), f32),
            jax.ShapeDtypeStruct((HIST_P, batch, W_B), f32),
            jax.ShapeDtypeStruct((streams, HIST_C, W_A), f32),
            jax.ShapeDtypeStruct((HIST_P, streams, W_B), f32),
        ],
        scratch_shapes=[
            pltpu.VMEM((HIST_C_PAD, W_A), f32), pltpu.VMEM((HIST_P_PAD, W_B), f32),
            pltpu.VMEM((D_MODEL, PROJ_W), bf16),
            pltpu.VMEM((2, 2 * GC, 2 * GC), bf16),
            pltpu.VMEM((D_MODEL, D_MODEL), bf16),
            pltpu.VMEM((D_MODEL, D_MODEL), bf16),
            pltpu.VMEM((PLE_DIM, D_MODEL), bf16),
            pltpu.VMEM((len(POOL_WINDOWS), GC, GC), f32),
            pltpu.SemaphoreType.DMA((tile // CAST_CHUNK_ROWS + 1,)),
        ],
        compiler_params=pltpu.CompilerParams(
            dimension_semantics=("arbitrary",),
            vmem_limit_bytes=VMEM_LIMIT_BYTES),
        name="trunk",
    )(x.reshape(batch * seq, D_MODEL), p.reshape(batch * seq, PLE_DIM),
      xs.reshape(streams * sample_seq, D_MODEL), ps.reshape(streams * sample_seq, PLE_DIM),
      cache_conv, cache_pool.transpose(1, 0, 2), *small_params, *matmul_weights)
    return (y.reshape(batch, seq, D_MODEL), ys.reshape(streams, sample_seq, D_MODEL),
            conv_p, pool_p.transpose(1, 0, 2), conv_s, pool_s.transpose(1, 0, 2))


def kernel(x_prompt, x_sample, cache_conv, cache_pool, p_prompt, p_sample, g_mix, w_in, conv_w,
           conv_b, pool_w, pool_scale, w_out, g_ple, w_ple_gate, w_ple, g_final):
    depth = g_mix.shape[0]
    assert depth == 1, "single-layer trunk"
    small_params = (g_mix, conv_w.reshape(1, CONV_W * W_A), conv_b, pool_scale, g_ple,
                    g_final.reshape(1, D_MODEL))
    matmul_weights = (w_in[0], pool_w[0], w_out[0], w_ple_gate[0], w_ple[0])
    y_prompt, y_sample, conv_p, pool_p, conv_s, pool_s = _run_trunk(
        x_prompt, p_prompt[0], x_sample, p_sample[0], cache_conv[0], cache_pool[0],
        small_params, matmul_weights)
    return y_prompt, y_sample, conv_p[None], pool_p[None], conv_s[None], pool_s[None]
```

```python
import functools

import jax
import jax.numpy as jnp
from jax.experimental import pallas as pl
from jax.experimental.pallas import tpu as pltpu

D_MODEL = 1024
W_A = 512
W_B = 512
CONV_W = 3
HIST_C = CONV_W - 1
POOL_WINDOWS = (2, 4, 8, 16)
GC = W_B // len(POOL_WINDOWS)
HIST_P = max(POOL_WINDOWS) - 1
PLE_DIM = 256
PROJ_W = 4 * W_A + 2 * W_B
EPS = 1e-6
PAST_LEN = 2048
LOG2_E = 1.4426950408889634

SUBLANES = 8
HIST_C_PAD = SUBLANES
HIST_P_PAD = 2 * SUBLANES
PROMPT_TILE = 1024
PROMPT_SUB_ROWS = 256
PROMPT_STAGE_SKEW = 2
CAST_CHUNK_ROWS = 128
N_SMALL_PARAMS = 6
N_MATMUL_WEIGHTS = 5
N_OUTPUTS = 6
VMEM_LIMIT_BYTES = 56 * 1024 * 1024


def _rmsnorm(x, g):
    ms = jnp.mean(x * x, axis=-1, keepdims=True)
    return x * jax.lax.rsqrt(ms + EPS) * g


def _sigmoid(z):
    return 1.0 / (1.0 + jnp.exp2(z * -LOG2_E))


def _silu(z):
    return z / (1.0 + jnp.exp2(z * -LOG2_E))


def _dot(a, w):
    return jnp.dot(a.astype(jnp.bfloat16), w, preferred_element_type=jnp.float32)


def _shift_rows(ext, k, pad):
    return pltpu.roll(ext, k, axis=0)[pad:]


def _conv_mixer(hA, bA, cA, zA, u_hist, conv_w, conv_b):
    u = cA * hA
    ext = jnp.concatenate([u_hist, u], axis=0)
    taps = [conv_w[:, k * W_A:(k + 1) * W_A] for k in range(CONV_W)]
    c = (_shift_rows(ext, 2, HIST_C_PAD) * taps[0] + _shift_rows(ext, 1, HIST_C_PAD) * taps[1]
         + u * taps[2] + conv_b)
    return bA * c * _silu(zA), u


def _pool_group(ext, v, window, pos0):
    rows = v.shape[0]
    total = ext.shape[0]
    s = ext
    span = 1
    while span < window:
        s = s + jnp.concatenate([s[total - span:], s[:total - span]], axis=0)
        span *= 2
    s = s[HIST_P_PAD:]
    head = HIST_P_PAD
    r = jax.lax.broadcasted_iota(jnp.int32, (head, GC), 0)
    cnt = jnp.minimum(pos0 + r + 1, window).astype(jnp.float32)
    mean_head = s[:head] / cnt
    if rows > head:
        mean = jnp.concatenate([mean_head, s[head:] * (1.0 / window)], axis=0)
    else:
        mean = mean_head
    return mean - v


def _pooled(vB, v_hist, pos0):
    pooled = []
    for g, window in enumerate(POOL_WINDOWS):
        sl = slice(g * GC, (g + 1) * GC)
        ext = jnp.concatenate([v_hist[:, sl], vB[:, sl]], axis=0)
        pooled.append(_pool_group(ext, vB[:, sl], window, pos0))
    return jnp.concatenate(pooled, axis=-1)


def _split_proj(proj):
    hA = proj[:, 0 * W_A:1 * W_A]
    bA = proj[:, 1 * W_A:2 * W_A]
    cA = proj[:, 2 * W_A:3 * W_A]
    zA = proj[:, 3 * W_A:4 * W_A]
    vB = proj[:, 4 * W_A:4 * W_A + W_B]
    zB = proj[:, 4 * W_A + W_B:]
    return hA, bA, cA, zA, vB, zB


class _Weights:
    def __init__(self, refs):
        (self.g_mix, self.w_in, self.conv_w, self.conv_b, self.pool_w, self.pool_scale,
         self.w_out, self.g_ple, self.w_gate, self.w_ple, self.g_final) = [r[...] for r in refs]


def _segment_stages(w, x, e, hists, pos0, carry, emit):
    s = {}

    def norm_in():
        s["hn"] = _rmsnorm(x, w.g_mix).astype(jnp.bfloat16)

    def project():
        s["proj"] = _dot(s.pop("hn"), w.w_in)

    def mix():
        hA, bA, cA, zA, vB, zB = _split_proj(s.pop("proj"))
        stream_hists = hists()
        seq = x.shape[0] // len(stream_hists)
        yA, pooled, tails = [], [], []
        for i, (u_hist, v_hist) in enumerate(stream_hists):
            sl = slice(i * seq, (i + 1) * seq)
            yA_i, u = _conv_mixer(hA[sl], bA[sl], cA[sl], zA[sl], u_hist, w.conv_w, w.conv_b)
            yA.append(yA_i)
            pooled.append(_pooled(vB[sl], v_hist, pos0))
            tails.append((u[seq - HIST_C_PAD:], vB[sl][seq - HIST_P_PAD:]))
        carry(tails)
        s["yA"] = jnp.concatenate(yA, axis=0).astype(jnp.bfloat16)
        s["pooled"] = jnp.concatenate(pooled, axis=0).astype(jnp.bfloat16)
        s["gB"] = _silu(zB)

    def project_pool():
        pooled = s.pop("pooled")
        s["mixed"] = [_dot(pooled[:, 2 * GC * i:2 * GC * (i + 1)], w.pool_w[i])
                      for i in range(len(POOL_WINDOWS) // 2)]

    def gate_pool():
        yB = jnp.concatenate(s.pop("mixed"), axis=-1) * w.pool_scale * s.pop("gB")
        s["y_mix"] = jnp.concatenate([s.pop("yA"), yB.astype(jnp.bfloat16)], axis=-1)

    def project_out():
        s["o"] = _dot(s.pop("y_mix"), w.w_out)

    def norm_gate():
        s["h"] = x + s.pop("o")
        s["hn2"] = _rmsnorm(s["h"], w.g_ple).astype(jnp.bfloat16)

    def project_gate():
        s["z"] = _dot(s.pop("hn2"), w.w_gate)

    def finish():
        h = s.pop("h") + e * _sigmoid(s.pop("z"))
        emit(_rmsnorm(h, w.g_final))

    return [norm_in, project, mix, project_pool, gate_pool, project_out, norm_gate,
            project_gate, finish]


def _run_skewed(stage_lists, skew):
    depth = len(stage_lists[0])
    for slot in range(skew * (len(stage_lists) - 1) + depth):
        for k, stages in enumerate(stage_lists):
            j = slot - skew * k
            if 0 <= j < depth:
                stages[j]()


def _cast_matmul_weights(hbm_refs, bf16_refs, stage_ref, pool_stage_ref, sems):
    w_in, pool_w, w_out, w_gate, w_ple = hbm_refs
    w_in_s, pool_w_s, w_out_s, w_gate_s, w_ple_s = bf16_refs
    stage_rows, width = stage_ref.shape
    n_buf = stage_rows // CAST_CHUNK_ROWS
    pool_copy = pltpu.make_async_copy(pool_w, pool_stage_ref, sems.at[n_buf])
    pool_copy.start()
    chunks = []
    for src, dst in ((w_in, w_in_s), (w_out, w_out_s), (w_gate, w_gate_s), (w_ple, w_ple_s)):
        for r in range(0, src.shape[0], CAST_CHUNK_ROWS):
            for c in range(0, src.shape[1], width):
                block = (pl.ds(r, CAST_CHUNK_ROWS), pl.ds(c, width))
                chunks.append((src.at[block], dst.at[block]))

    def slot_ref(i):
        return stage_ref.at[pl.ds((i % n_buf) * CAST_CHUNK_ROWS, CAST_CHUNK_ROWS)]

    def copy(i):
        return pltpu.make_async_copy(chunks[i][0], slot_ref(i), sems.at[i % n_buf])

    for i in range(min(n_buf - 1, len(chunks))):
        copy(i).start()
    for i, (_, dst) in enumerate(chunks):
        if i + n_buf - 1 < len(chunks):
            copy(i + n_buf - 1).start()
        copy(i).wait()
        dst[...] = slot_ref(i)[...].astype(jnp.bfloat16)
    pool_copy.wait()
    pool_w_s[...] = jnp.zeros_like(pool_w_s)
    for g in range(len(POOL_WINDOWS)):
        pair, half = divmod(g, 2)
        block = slice(half * GC, (half + 1) * GC)
        pool_w_s[pair, block, block] = pool_stage_ref[g].astype(jnp.bfloat16)


def _trunk_kernel(x_ref, p_ref, xs_ref, ps_ref, cache_conv_ref, cache_pool_ref, *refs,
                  sub_rows, tiles_per_seq, n_prompt_tiles, sample_seq, sample_pos0):
    refs = list(refs)
    take = lambda n: [refs.pop(0) for _ in range(n)]
    g_mix_ref, conv_w_ref, conv_b_ref, pool_scale_ref, g_ple_ref, g_final_ref = take(N_SMALL_PARAMS)
    hbm_weight_refs = take(N_MATMUL_WEIGHTS)
    y_ref, ys_ref, conv_p_ref, pool_p_ref, conv_s_ref, pool_s_ref = take(N_OUTPUTS)
    u_hist_ref, v_hist_ref = take(2)
    bf16_weight_refs = take(N_MATMUL_WEIGHTS)
    pool_stage_ref, sems = refs
    w_in_ref, pool_w_ref, w_out_ref, w_gate_ref, w_ple_ref = bf16_weight_refs
    weight_refs = (g_mix_ref, w_in_ref, conv_w_ref, conv_b_ref, pool_w_ref, pool_scale_ref,
                   w_out_ref, g_ple_ref, w_gate_ref, w_ple_ref, g_final_ref)
    t = pl.program_id(0)
    tile_in_seq = jax.lax.rem(t, tiles_per_seq)

    @pl.when(t == 0)
    def _():
        _cast_matmul_weights(hbm_weight_refs, bf16_weight_refs, y_ref, pool_stage_ref, sems)

    @pl.when(jnp.logical_and(t < n_prompt_tiles, tile_in_seq == 0))
    def _():
        u_hist_ref[...] = jnp.zeros_like(u_hist_ref)
        v_hist_ref[...] = jnp.zeros_like(v_hist_ref)

    @pl.when(t < n_prompt_tiles)
    def _():
        rows = x_ref.shape[0]
        w = _Weights(weight_refs)
        e = _dot(p_ref[...], w.w_ple)
        tails = [(u_hist_ref[...], v_hist_ref[...])]
        stage_lists = []
        for k in range(rows // sub_rows):
            sl = pl.ds(k * sub_rows, sub_rows)

            def emit(out, sl=sl):
                y_ref[sl, :] = out

            stage_lists.append(_segment_stages(
                w, x_ref[sl, :], e[k * sub_rows:(k + 1) * sub_rows], lambda k=k: [tails[k]],
                tile_in_seq * rows + k * sub_rows, lambda new: tails.append(new[0]), emit))
        _run_skewed(stage_lists, skew=PROMPT_STAGE_SKEW)
        u_tail, v_tail = tails[-1]
        u_hist_ref[...] = u_tail
        v_hist_ref[...] = v_tail
        seq = jax.lax.div(t, tiles_per_seq)
        conv_p_ref[seq] = u_tail[HIST_C_PAD - HIST_C:]
        for b in range(pool_p_ref.shape[1]):
            @pl.when(seq == b)
            def _(b=b):
                pool_p_ref[:, b, :] = v_tail[HIST_P_PAD - HIST_P:]

    @pl.when(t >= n_prompt_tiles)
    def _():
        streams = xs_ref.shape[0] // sample_seq

        def hists():
            zeros_c = jnp.zeros((HIST_C_PAD - HIST_C, W_A), jnp.float32)
            zeros_p = jnp.zeros((HIST_P_PAD - HIST_P, W_B), jnp.float32)
            return [(jnp.concatenate([zeros_c, cache_conv_ref[i]], axis=0),
                     jnp.concatenate([zeros_p, cache_pool_ref[:, i, :]], axis=0))
                    for i in range(streams)]

        def carry(tails):
            for i, (u_tail, v_tail) in enumerate(tails):
                conv_s_ref[i] = u_tail[HIST_C_PAD - HIST_C:]
                pool_s_ref[:, i, :] = v_tail[HIST_P_PAD - HIST_P:]

        def emit(out):
            ys_ref[...] = out

        w = _Weights(weight_refs)
        stages = _segment_stages(w, xs_ref[...], _dot(ps_ref[...], w.w_ple), hists, sample_pos0,
                                 carry, emit)
        _run_skewed([stages], skew=1)


def _resident(shape):
    zeros = (0,) * len(shape)
    return pl.BlockSpec(shape, lambda *_: zeros, pipeline_mode=pl.Buffered(1))


def _small_param_specs():
    return [_resident((1, D_MODEL)), _resident((1, CONV_W * W_A)), _resident((1, W_A)),
            _resident((1, W_B)), _resident((1, D_MODEL)), _resident((1, D_MODEL))]


def _run_trunk(x, p, xs, ps, cache_conv, cache_pool, small_params, matmul_weights):
    batch, seq, _ = x.shape
    streams, sample_seq, _ = xs.shape
    tile, sub_rows = PROMPT_TILE, PROMPT_SUB_ROWS
    assert seq % tile == 0 and tile % sub_rows == 0 and sub_rows >= HIST_P_PAD
    assert sub_rows % sample_seq == 0 and sample_seq >= HIST_P_PAD and sample_seq % SUBLANES == 0
    assert tile % CAST_CHUNK_ROWS == 0 and all(
        w.shape[0] % CAST_CHUNK_ROWS == 0 and w.shape[1] % D_MODEL == 0
        for w in matmul_weights if w.ndim == 2), "weights are staged in output-block row slots"
    per_tile = sub_rows // sample_seq
    assert streams % per_tile == 0
    n_prompt_tiles = batch * seq // tile
    n_sample_tiles = streams // per_tile
    prompt_tile = lambda t: (jnp.minimum(t, n_prompt_tiles - 1), 0)
    sample_tile = lambda t: (jnp.clip(t - n_prompt_tiles, 0, n_sample_tiles - 1), 0)
    sample_streams = lambda t: sample_tile(t) + (0,)
    sample_rows_streams = lambda t: (0,) + sample_tile(t)
    whole = lambda t: (0, 0, 0)
    f32, bf16 = jnp.float32, jnp.bfloat16
    y, ys, conv_p, pool_p, conv_s, pool_s = pl.pallas_call(
        functools.partial(_trunk_kernel, sub_rows=sub_rows, tiles_per_seq=seq // tile,
                          n_prompt_tiles=n_prompt_tiles, sample_seq=sample_seq,
                          sample_pos0=PAST_LEN),
        grid=(n_prompt_tiles + n_sample_tiles,),
        in_specs=[pl.BlockSpec((tile, D_MODEL), prompt_tile),
                  pl.BlockSpec((tile, PLE_DIM), prompt_tile),
                  pl.BlockSpec((sub_rows, D_MODEL), sample_tile),
                  pl.BlockSpec((sub_rows, PLE_DIM), sample_tile),
                  pl.BlockSpec((per_tile, HIST_C, W_A), sample_streams),
                  pl.BlockSpec((HIST_P, per_tile, W_B), sample_rows_streams)]
        + _small_param_specs() + [pl.BlockSpec(memory_space=pl.ANY)] * len(matmul_weights),
        out_specs=[pl.BlockSpec((tile, D_MODEL), prompt_tile),
                   pl.BlockSpec((sub_rows, D_MODEL), sample_tile),
                   pl.BlockSpec((batch, HIST_C, W_A), whole),
                   pl.BlockSpec((HIST_P, batch, W_B), whole),
                   pl.BlockSpec((per_tile, HIST_C, W_A), sample_streams),
                   pl.BlockSpec((HIST_P, per_tile, W_B), sample_rows_streams)],
        out_shape=[
            jax.ShapeDtypeStruct((batch * seq, D_MODEL), f32),
            jax.ShapeDtypeStruct((streams * sample_seq, D_MODEL), f32),
            jax.ShapeDtypeStruct((batch, HIST_C, W_A), f32),
            jax.ShapeDtypeStruct((HIST_P, batch, W_B), f32),
            jax.ShapeDtypeStruct((streams, HIST_C, W_A), f32),
            jax.ShapeDtypeStruct((HIST_P, streams, W_B), f32),
        ],
        scratch_shapes=[
            pltpu.VMEM((HIST_C_PAD, W_A), f32), pltpu.VMEM((HIST_P_PAD, W_B), f32),
            pltpu.VMEM((D_MODEL, PROJ_W), bf16),
            pltpu.VMEM((2, 2 * GC, 2 * GC), bf16),
            pltpu.VMEM((D_MODEL, D_MODEL), bf16),
            pltpu.VMEM((D_MODEL, D_MODEL), bf16),
            pltpu.VMEM((PLE_DIM, D_MODEL), bf16),
            pltpu.VMEM((len(POOL_WINDOWS), GC, GC), f32),
            pltpu.SemaphoreType.DMA((tile // CAST_CHUNK_ROWS + 1,)),
        ],
        compiler_params=pltpu.CompilerParams(
            dimension_semantics=("arbitrary",),
            vmem_limit_bytes=VMEM_LIMIT_BYTES),
        name="trunk",
    )(x.reshape(batch * seq, D_MODEL), p.reshape(batch * seq, PLE_DIM),
      xs.reshape(streams * sample_seq, D_MODEL), ps.reshape(streams * sample_seq, PLE_DIM),
      cache_conv, cache_pool.transpose(1, 0, 2), *small_params, *matmul_weights)
    return (y.reshape(batch, seq, D_MODEL), ys.reshape(streams, sample_seq, D_MODEL),
            conv_p, pool_p.transpose(1, 0, 2), conv_s, pool_s.transpose(1, 0, 2))


def kernel(x_prompt, x_sample, cache_conv, cache_pool, p_prompt, p_sample, g_mix, w_in, conv_w,
           conv_b, pool_w, pool_scale, w_out, g_ple, w_ple_gate, w_ple, g_final):
    depth = g_mix.shape[0]
    assert depth == 1, "single-layer trunk"
    small_params = (g_mix, conv_w.reshape(1, CONV_W * W_A), conv_b, pool_scale, g_ple,
                    g_final.reshape(1, D_MODEL))
    matmul_weights = (w_in[0], pool_w[0], w_out[0], w_ple_gate[0], w_ple[0])
    y_prompt, y_sample, conv_p, pool_p, conv_s, pool_s = _run_trunk(
        x_prompt, p_prompt[0], x_sample, p_sample[0], cache_conv[0], cache_pool[0],
        small_params, matmul_weights)
    return y_prompt, y_sample, conv_p[None], pool_p[None], conv_s[None], pool_s[None]
```

```python
import functools

import jax
import jax.numpy as jnp
from jax.experimental import pallas as pl
from jax.experimental.pallas import tpu as pltpu

D_MODEL = 1024
W_A = 512
W_B = 512
CONV_W = 3
HIST_C = CONV_W - 1
POOL_WINDOWS = (2, 4, 8, 16)
GC = W_B // len(POOL_WINDOWS)
HIST_P = max(POOL_WINDOWS) - 1
PLE_DIM = 256
PROJ_W = 4 * W_A + 2 * W_B
EPS = 1e-6
PAST_LEN = 2048
LOG2_E = 1.4426950408889634

SUBLANES = 8
HIST_C_PAD = SUBLANES
HIST_P_PAD = 2 * SUBLANES
PROMPT_TILE = 1024
PROMPT_SUB_ROWS = 256
PROMPT_STAGE_SKEW = 2
CAST_CHUNK_ROWS = 128
N_SMALL_PARAMS = 6
N_MATMUL_WEIGHTS = 5
N_OUTPUTS = 6
VMEM_LIMIT_BYTES = 56 * 1024 * 1024


def _rmsnorm(x, g):
    ms = jnp.mean(x * x, axis=-1, keepdims=True)
    return x * jax.lax.rsqrt(ms + EPS) * g


def _sigmoid(z):
    return 1.0 / (1.0 + jnp.exp2(z * -LOG2_E))


def _silu(z):
    return z / (1.0 + jnp.exp2(z * -LOG2_E))


def _dot(a, w):
    return jnp.dot(a.astype(jnp.bfloat16), w, preferred_element_type=jnp.float32)


def _shift_rows(ext, k, pad):
    return pltpu.roll(ext, k, axis=0)[pad:]


def _conv_mixer(hA, bA, cA, zA, u_hist, conv_w, conv_b):
    u = cA * hA
    ext = jnp.concatenate([u_hist, u], axis=0)
    taps = [conv_w[:, k * W_A:(k + 1) * W_A] for k in range(CONV_W)]
    c = (_shift_rows(ext, 2, HIST_C_PAD) * taps[0] + _shift_rows(ext, 1, HIST_C_PAD) * taps[1]
         + u * taps[2] + conv_b)
    return bA * c * _silu(zA), u


def _pool_group(ext, v, window, pos0):
    rows = v.shape[0]
    total = ext.shape[0]
    s = ext
    span = 1
    while span < window:
        s = s + jnp.concatenate([s[total - span:], s[:total - span]], axis=0)
        span *= 2
    s = s[HIST_P_PAD:]
    head = HIST_P_PAD
    r = jax.lax.broadcasted_iota(jnp.int32, (head, GC), 0)
    cnt = jnp.minimum(pos0 + r + 1, window).astype(jnp.float32)
    mean_head = s[:head] / cnt
    if rows > head:
        mean = jnp.concatenate([mean_head, s[head:] * (1.0 / window)], axis=0)
    else:
        mean = mean_head
    return mean - v


def _pooled(vB, v_hist, pos0):
    pooled = []
    for g, window in enumerate(POOL_WINDOWS):
        sl = slice(g * GC, (g + 1) * GC)
        ext = jnp.concatenate([v_hist[:, sl], vB[:, sl]], axis=0)
        pooled.append(_pool_group(ext, vB[:, sl], window, pos0))
    return jnp.concatenate(pooled, axis=-1)


def _split_proj(proj):
    hA = proj[:, 0 * W_A:1 * W_A]
    bA = proj[:, 1 * W_A:2 * W_A]
    cA = proj[:, 2 * W_A:3 * W_A]
    zA = proj[:, 3 * W_A:4 * W_A]
    vB = proj[:, 4 * W_A:4 * W_A + W_B]
    zB = proj[:, 4 * W_A + W_B:]
    return hA, bA, cA, zA, vB, zB


class _Weights:
    def __init__(self, refs):
        (self.g_mix, self.w_in, self.conv_w, self.conv_b, self.pool_w, self.pool_scale,
         self.w_out, self.g_ple, self.w_gate, self.w_ple, self.g_final) = [r[...] for r in refs]


def _segment_stages(w, x, e, hists, pos0, carry, emit):
    s = {}

    def norm_in():
        s["hn"] = _rmsnorm(x, w.g_mix).astype(jnp.bfloat16)

    def project():
        s["proj"] = _dot(s.pop("hn"), w.w_in)

    def mix():
        hA, bA, cA, zA, vB, zB = _split_proj(s.pop("proj"))
        stream_hists = hists()
        seq = x.shape[0] // len(stream_hists)
        yA, pooled, tails = [], [], []
        for i, (u_hist, v_hist) in enumerate(stream_hists):
            sl = slice(i * seq, (i + 1) * seq)
            yA_i, u = _conv_mixer(hA[sl], bA[sl], cA[sl], zA[sl], u_hist, w.conv_w, w.conv_b)
            yA.append(yA_i)
            pooled.append(_pooled(vB[sl], v_hist, pos0))
            tails.append((u[seq - HIST_C_PAD:], vB[sl][seq - HIST_P_PAD:]))
        carry(tails)
        s["yA"] = jnp.concatenate(yA, axis=0).astype(jnp.bfloat16)
        s["pooled"] = jnp.concatenate(pooled, axis=0).astype(jnp.bfloat16)
        s["gB"] = _silu(zB)

    def project_pool():
        pooled = s.pop("pooled")
        s["mixed"] = [_dot(pooled[:, 2 * GC * i:2 * GC * (i + 1)], w.pool_w[i])
                      for i in range(len(POOL_WINDOWS) // 2)]

    def gate_pool():
        yB = jnp.concatenate(s.pop("mixed"), axis=-1) * w.pool_scale * s.pop("gB")
        s["y_mix"] = jnp.concatenate([s.pop("yA"), yB.astype(jnp.bfloat16)], axis=-1)

    def project_out():
        s["o"] = _dot(s.pop("y_mix"), w.w_out)

    def norm_gate():
        s["h"] = x + s.pop("o")
        s["hn2"] = _rmsnorm(s["h"], w.g_ple).astype(jnp.bfloat16)

    def project_gate():
        s["z"] = _dot(s.pop("hn2"), w.w_gate)

    def finish():
        h = s.pop("h") + e * _sigmoid(s.pop("z"))
        emit(_rmsnorm(h, w.g_final))

    return [norm_in, project, mix, project_pool, gate_pool, project_out, norm_gate,
            project_gate, finish]


def _run_skewed(stage_lists, skew):
    depth = len(stage_lists[0])
    for slot in range(skew * (len(stage_lists) - 1) + depth):
        for k, stages in enumerate(stage_lists):
            j = slot - skew * k
            if 0 <= j < depth:
                stages[j]()


def _cast_matmul_weights(hbm_refs, bf16_refs, stage_ref, pool_stage_ref, sems):
    w_in, pool_w, w_out, w_gate, w_ple = hbm_refs
    w_in_s, pool_w_s, w_out_s, w_gate_s, w_ple_s = bf16_refs
    stage_rows, width = stage_ref.shape
    n_buf = stage_rows // CAST_CHUNK_ROWS
    pool_copy = pltpu.make_async_copy(pool_w, pool_stage_ref, sems.at[n_buf])
    pool_copy.start()
    chunks = []
    for src, dst in ((w_in, w_in_s), (w_out, w_out_s), (w_gate, w_gate_s), (w_ple, w_ple_s)):
        for r in range(0, src.shape[0], CAST_CHUNK_ROWS):
            for c in range(0, src.shape[1], width):
                block = (pl.ds(r, CAST_CHUNK_ROWS), pl.ds(c, width))
                chunks.append((src.at[block], dst.at[block]))

    def slot_ref(i):
        return stage_ref.at[pl.ds((i % n_buf) * CAST_CHUNK_ROWS, CAST_CHUNK_ROWS)]

    def copy(i):
        return pltpu.make_async_copy(chunks[i][0], slot_ref(i), sems.at[i % n_buf])

    for i in range(min(n_buf - 1, len(chunks))):
        copy(i).start()
    for i, (_, dst) in enumerate(chunks):
        if i + n_buf - 1 < len(chunks):
            copy(i + n_buf - 1).start()
        copy(i).wait()
        dst[...] = slot_ref(i)[...].astype(jnp.bfloat16)
    pool_copy.wait()
    pool_w_s[...] = jnp.zeros_like(pool_w_s)
    for g in range(len(POOL_WINDOWS)):
        pair, half = divmod(g, 2)
        block = slice(half * GC, (half + 1) * GC)
        pool_w_s[pair, block, block] = pool_stage_ref[g].astype(jnp.bfloat16)


def _trunk_kernel(x_ref, p_ref, xs_ref, ps_ref, cache_conv_ref, cache_pool_ref, *refs,
                  sub_rows, tiles_per_seq, n_prompt_tiles, sample_seq, sample_pos0):
    refs = list(refs)
    take = lambda n: [refs.pop(0) for _ in range(n)]
    g_mix_ref, conv_w_ref, conv_b_ref, pool_scale_ref, g_ple_ref, g_final_ref = take(N_SMALL_PARAMS)
    hbm_weight_refs = take(N_MATMUL_WEIGHTS)
    y_ref, ys_ref, conv_p_ref, pool_p_ref, conv_s_ref, pool_s_ref = take(N_OUTPUTS)
    u_hist_ref, v_hist_ref = take(2)
    bf16_weight_refs = take(N_MATMUL_WEIGHTS)
    pool_stage_ref, sems = refs
    w_in_ref, pool_w_ref, w_out_ref, w_gate_ref, w_ple_ref = bf16_weight_refs
    weight_refs = (g_mix_ref, w_in_ref, conv_w_ref, conv_b_ref, pool_w_ref, pool_scale_ref,
                   w_out_ref, g_ple_ref, w_gate_ref, w_ple_ref, g_final_ref)
    t = pl.program_id(0)
    tile_in_seq = jax.lax.rem(t, tiles_per_seq)

    @pl.when(t == 0)
    def _():
        _cast_matmul_weights(hbm_weight_refs, bf16_weight_refs, y_ref, pool_stage_ref, sems)

    @pl.when(jnp.logical_and(t < n_prompt_tiles, tile_in_seq == 0))
    def _():
        u_hist_ref[...] = jnp.zeros_like(u_hist_ref)
        v_hist_ref[...] = jnp.zeros_like(v_hist_ref)

    @pl.when(t < n_prompt_tiles)
    def _():
        rows = x_ref.shape[0]
        w = _Weights(weight_refs)
        segments = [pl.ds(k * sub_rows, sub_rows) for k in range(rows // sub_rows)]
        es = [_dot(p_ref[sl, :], w.w_ple) for sl in segments]
        tails = [(u_hist_ref[...], v_hist_ref[...])]
        stage_lists = []
        for k, sl in enumerate(segments):

            def emit(out, sl=sl):
                y_ref[sl, :] = out

            stage_lists.append(_segment_stages(
                w, x_ref[sl, :], es[k], lambda k=k: [tails[k]],
                tile_in_seq * rows + k * sub_rows, lambda new: tails.append(new[0]), emit))
        _run_skewed(stage_lists, skew=PROMPT_STAGE_SKEW)
        u_tail, v_tail = tails[-1]
        u_hist_ref[...] = u_tail
        v_hist_ref[...] = v_tail
        seq = jax.lax.div(t, tiles_per_seq)
        conv_p_ref[seq] = u_tail[HIST_C_PAD - HIST_C:]
        for b in range(pool_p_ref.shape[1]):
            @pl.when(seq == b)
            def _(b=b):
                pool_p_ref[:, b, :] = v_tail[HIST_P_PAD - HIST_P:]

    @pl.when(t >= n_prompt_tiles)
    def _():
        streams = xs_ref.shape[0] // sample_seq

        def hists():
            zeros_c = jnp.zeros((HIST_C_PAD - HIST_C, W_A), jnp.float32)
            zeros_p = jnp.zeros((HIST_P_PAD - HIST_P, W_B), jnp.float32)
            return [(jnp.concatenate([zeros_c, cache_conv_ref[i]], axis=0),
                     jnp.concatenate([zeros_p, cache_pool_ref[:, i, :]], axis=0))
                    for i in range(streams)]

        def carry(tails):
            for i, (u_tail, v_tail) in enumerate(tails):
                conv_s_ref[i] = u_tail[HIST_C_PAD - HIST_C:]
                pool_s_ref[:, i, :] = v_tail[HIST_P_PAD - HIST_P:]

        def emit(out):
            ys_ref[...] = out

        w = _Weights(weight_refs)
        stages = _segment_stages(w, xs_ref[...], _dot(ps_ref[...], w.w_ple), hists, sample_pos0,
                                 carry, emit)
        _run_skewed([stages], skew=1)


def _resident(shape):
    zeros = (0,) * len(shape)
    return pl.BlockSpec(shape, lambda *_: zeros, pipeline_mode=pl.Buffered(1))


def _small_param_specs():
    return [_resident((1, D_MODEL)), _resident((1, CONV_W * W_A)), _resident((1, W_A)),
            _resident((1, W_B)), _resident((1, D_MODEL)), _resident((1, D_MODEL))]


def _run_trunk(x, p, xs, ps, cache_conv, cache_pool, small_params, matmul_weights):
    batch, seq, _ = x.shape
    streams, sample_seq, _ = xs.shape
    tile, sub_rows = PROMPT_TILE, PROMPT_SUB_ROWS
    assert seq % tile == 0 and tile % sub_rows == 0 and sub_rows >= HIST_P_PAD
    assert sub_rows % sample_seq == 0 and sample_seq >= HIST_P_PAD and sample_seq % SUBLANES == 0
    assert tile % CAST_CHUNK_ROWS == 0 and all(
        w.shape[0] % CAST_CHUNK_ROWS == 0 and w.shape[1] % D_MODEL == 0
        for w in matmul_weights if w.ndim == 2), "weights are staged in output-block row slots"
    per_tile = sub_rows // sample_seq
    assert streams % per_tile == 0
    n_prompt_tiles = batch * seq // tile
    n_sample_tiles = streams // per_tile
    prompt_tile = lambda t: (jnp.minimum(t, n_prompt_tiles - 1), 0)
    sample_tile = lambda t: (jnp.clip(t - n_prompt_tiles, 0, n_sample_tiles - 1), 0)
    sample_streams = lambda t: sample_tile(t) + (0,)
    sample_rows_streams = lambda t: (0,) + sample_tile(t)
    whole = lambda t: (0, 0, 0)
    f32, bf16 = jnp.float32, jnp.bfloat16
    y, ys, conv_p, pool_p, conv_s, pool_s = pl.pallas_call(
        functools.partial(_trunk_kernel, sub_rows=sub_rows, tiles_per_seq=seq // tile,
                          n_prompt_tiles=n_prompt_tiles, sample_seq=sample_seq,
                          sample_pos0=PAST_LEN),
        grid=(n_prompt_tiles + n_sample_tiles,),
        in_specs=[pl.BlockSpec((tile, D_MODEL), prompt_tile),
                  pl.BlockSpec((tile, PLE_DIM), prompt_tile),
                  pl.BlockSpec((sub_rows, D_MODEL), sample_tile),
                  pl.BlockSpec((sub_rows, PLE_DIM), sample_tile),
                  pl.BlockSpec((per_tile, HIST_C, W_A), sample_streams),
                  pl.BlockSpec((HIST_P, per_tile, W_B), sample_rows_streams)]
        + _small_param_specs() + [pl.BlockSpec(memory_space=pl.ANY)] * len(matmul_weights),
        out_specs=[pl.BlockSpec((tile, D_MODEL), prompt_tile),
                   pl.BlockSpec((sub_rows, D_MODEL), sample_tile),
                   pl.BlockSpec((batch, HIST_C, W_A), whole),
                   pl.BlockSpec((HIST_P, batch, W_B), whole),
                   pl.BlockSpec((per_tile, HIST_C, W_A), sample_streams),
                   pl.BlockSpec((HIST_P, per_tile, W_B), sample_rows_streams)],
        out_shape=[
            jax.ShapeDtypeStruct((batch * seq, D_MODEL), f32),
            jax.ShapeDtypeStruct((streams * sample_seq, D_MODEL), f32),
            jax.ShapeDtypeStruct((batch, HIST_C, W_A), f32),
            jax.ShapeDtypeStruct((HIST_P, batch, W_B), f32),
            jax.ShapeDtypeStruct((streams, HIST_C, W_A), f32),
            jax.ShapeDtypeStruct((HIST_P, streams, W_B), f32),
        ],
        scratch_shapes=[
            pltpu.VMEM((HIST_C_PAD, W_A), f32), pltpu.VMEM((HIST_P_PAD, W_B), f32),
            pltpu.VMEM((D_MODEL, PROJ_W), bf16),
            pltpu.VMEM((2, 2 * GC, 2 * GC), bf16),
            pltpu.VMEM((D_MODEL, D_MODEL), bf16),
            pltpu.VMEM((D_MODEL, D_MODEL), bf16),
            pltpu.VMEM((PLE_DIM, D_MODEL), bf16),
            pltpu.VMEM((len(POOL_WINDOWS), GC, GC), f32),
            pltpu.SemaphoreType.DMA((tile // CAST_CHUNK_ROWS + 1,)),
        ],
        compiler_params=pltpu.CompilerParams(
            dimension_semantics=("arbitrary",),
            vmem_limit_bytes=VMEM_LIMIT_BYTES),
        name="trunk",
    )(x.reshape(batch * seq, D_MODEL), p.reshape(batch * seq, PLE_DIM),
      xs.reshape(streams * sample_seq, D_MODEL), ps.reshape(streams * sample_seq, PLE_DIM),
      cache_conv, cache_pool.transpose(1, 0, 2), *small_params, *matmul_weights)
    return (y.reshape(batch, seq, D_MODEL), ys.reshape(streams, sample_seq, D_MODEL),
            conv_p, pool_p.transpose(1, 0, 2), conv_s, pool_s.transpose(1, 0, 2))


def kernel(x_prompt, x_sample, cache_conv, cache_pool, p_prompt, p_sample, g_mix, w_in, conv_w,
           conv_b, pool_w, pool_scale, w_out, g_ple, w_ple_gate, w_ple, g_final):
    depth = g_mix.shape[0]
    assert depth == 1, "single-layer trunk"
    small_params = (g_mix, conv_w.reshape(1, CONV_W * W_A), conv_b, pool_scale, g_ple,
                    g_final.reshape(1, D_MODEL))
    matmul_weights = (w_in[0], pool_w[0], w_out[0], w_ple_gate[0], w_ple[0])
    y_prompt, y_sample, conv_p, pool_p, conv_s, pool_s = _run_trunk(
        x_prompt, p_prompt[0], x_sample, p_sample[0], cache_conv[0], cache_pool[0],
        small_params, matmul_weights)
    return y_prompt, y_sample, conv_p[None], pool_p[None], conv_s[None], pool_s[None]
```
